```python
import jax, jax.numpy as jnp
from jax import lax
import numpy as np

D_MODEL = 1024
BATCH = 8
SEQ = 2048
DEPTH = 1
DEC_BATCH = 128
DEC_SEQ = 1
PAST_LEN = 16384
PAGE_SIZE = 128

N_META = 16
ROPE_THETA = 500000.0
EPS = 1e-6
Q_BLOCK = 128
MLA_HEADS = 8
MLA_Q_LORA = 384
MLA_KV_LORA = 256
MLA_NOPE = 64
MLA_ROPE = 32
MLA_QK = MLA_NOPE + MLA_ROPE
MLA_V = 64
MLA_SCALE = MLA_QK ** -0.5
DSA_HEADS = 8
DSA_KV_HEADS = 4
DSA_HEAD_DIM = 64
DSA_ROT = DSA_HEAD_DIM // 4
DSA_SCALE = DSA_HEAD_DIM ** -0.5
IDX_HEADS = 8
IDX_DIM = 64
IDX_ROT = IDX_DIM // 4
IDX_SCALE = (IDX_DIM * IDX_HEADS) ** -0.5
TOPK_MAX = 256
D_FF = ((8 * D_MODEL + 3 * 256 - 1) // (3 * 256)) * 256
IN_SPLITS = (MLA_Q_LORA, MLA_KV_LORA, MLA_ROPE, DSA_HEADS * DSA_HEAD_DIM, DSA_KV_HEADS * DSA_HEAD_DIM, DSA_KV_HEADS * DSA_HEAD_DIM, IDX_HEADS * IDX_DIM, IDX_DIM, IDX_HEADS, D_MODEL, D_MODEL)
IN_WIDTH = sum(IN_SPLITS)

kernel_name = "hybrid_mla_dsa_gated_decode_step"


def rmsnorm(x, g):
    xf = x.astype(jnp.float32)
    y = xf * lax.rsqrt(jnp.mean(xf * xf, axis=-1, keepdims=True) + EPS)
    return (y * g.astype(jnp.float32)).astype(x.dtype)


def rope(x, pos, rot):
    half = rot // 2
    inv_freq = ROPE_THETA ** (-jnp.arange(half, dtype=jnp.float32) / half)
    ang = pos.astype(jnp.float32)[:, None] * inv_freq[None, :]
    cos = jnp.cos(ang)[:, None, :].astype(x.dtype)
    sin = jnp.sin(ang)[:, None, :].astype(x.dtype)
    x1, x2 = x[..., :half], x[..., half:rot]
    return jnp.concatenate([x1 * cos - x2 * sin, x2 * cos + x1 * sin, x[..., rot:]], axis=-1)


def split_points():
    pts, acc = [], 0
    for w in IN_SPLITS[:-1]:
        acc += w
        pts.append(acc)
    return pts


def mixer_inputs(h, pos, w_in, g_q_lat, w_uq, g_mla_q, g_kv_lat, g_dsa_q, g_dsa_k):
    B, T, _ = h.shape
    z = jnp.einsum('btd,de->bte', h, w_in)
    (q_lat, kv_lat, k_pe, q_d, k_d, v_d, q_i, k_i, w_i, gate_a, gate_b) = jnp.split(z, split_points(), axis=-1)
    q_m = jnp.einsum('btr,rhd->bthd', rmsnorm(q_lat, g_q_lat), w_uq)
    q_m = rope(rmsnorm(q_m, g_mla_q), pos, MLA_ROPE)
    c_kv = rmsnorm(kv_lat, g_kv_lat)
    q_d = rope(rmsnorm(q_d.reshape(B, T, DSA_HEADS, DSA_HEAD_DIM), g_dsa_q), pos, DSA_ROT)
    k_d = rope(rmsnorm(k_d.reshape(B, T, DSA_KV_HEADS, DSA_HEAD_DIM), g_dsa_k), pos, DSA_ROT)
    v_d = v_d.reshape(B, T, DSA_KV_HEADS, DSA_HEAD_DIM)
    q_i = rope(q_i.reshape(B, T, IDX_HEADS, IDX_DIM), pos, IDX_ROT)
    k_i = rope(k_i[:, :, None, :], pos, IDX_ROT)[:, :, 0, :]
    return q_m, c_kv, k_pe, q_d, k_d, v_d, q_i, k_i, w_i, gate_a, gate_b


def mla_keys_values(c_kv, k_pe, pos, w_uk, w_uv, g_mla_k):
    k_nope = jnp.einsum('...sc,chd->...shd', c_kv, w_uk)
    k_rot = jnp.broadcast_to(k_pe[..., None, :], k_nope.shape[:-1] + (MLA_ROPE,))
    k = rope(rmsnorm(jnp.concatenate([k_rot, k_nope], axis=-1), g_mla_k), pos, MLA_ROPE)
    v = jnp.einsum('...sc,chd->...shd', c_kv, w_uv)
    return k, v


def indexer_scores(q_i, k_i, w_i):
    dots = jnp.einsum('...qhd,...kd->...qhk', q_i, k_i).astype(jnp.float32)
    return jnp.einsum('...qhk,...qh->...qk', jax.nn.relu(dots), w_i.astype(jnp.float32)) * IDX_SCALE


def sparse_attend(q, k_sel, v_sel, valid):
    lead = q.shape[:-2]
    qg = q.reshape(lead + (DSA_KV_HEADS, DSA_HEADS // DSA_KV_HEADS, DSA_HEAD_DIM))
    s = jnp.einsum('...qngd,...qsnd->...qngs', qg, k_sel).astype(jnp.float32) * DSA_SCALE
    s = jnp.where(valid[..., :, None, None, :], s, -jnp.inf)
    p = jax.nn.softmax(s, axis=-1).astype(v_sel.dtype)
    o = jnp.einsum('...qngs,...qsnd->...qngd', p, v_sel)
    return o.reshape(lead + (DSA_HEADS * DSA_HEAD_DIM,))


def to_query_blocks(a, nb):
    pad = [(0, 0), (0, nb * Q_BLOCK - a.shape[1])] + [(0, 0)] * (a.ndim - 2)
    a = jnp.pad(a, pad)
    a = a.reshape((a.shape[0], nb, Q_BLOCK) + a.shape[2:])
    return jnp.moveaxis(a, 1, 0)


def from_query_blocks(o, T):
    o = jnp.moveaxis(o, 0, 1)
    return o.reshape((o.shape[0], -1) + o.shape[3:])[:, :T]


def mla_prompt_attention(q, k, v):
    B, T = q.shape[:2]
    nb = -(-T // Q_BLOCK)
    kpos = jnp.arange(T)

    def block(args):
        qb, i = args
        qpos = i * Q_BLOCK + jnp.arange(Q_BLOCK)
        s = jnp.einsum('bqhd,bkhd->bhqk', qb, k).astype(jnp.float32) * MLA_SCALE
        s = jnp.where(kpos[None, :] <= qpos[:, None], s, -jnp.inf)
        p = jax.nn.softmax(s, axis=-1).astype(v.dtype)
        return jnp.einsum('bhqk,bkhd->bqhd', p, v)

    o = lax.map(block, (to_query_blocks(q, nb), jnp.arange(nb)))
    return from_query_blocks(o, T).reshape(B, T, MLA_HEADS * MLA_V)


def dsa_prompt_attention(q_d, k_d, v_d, q_i, k_i, w_i, topk):
    T = q_d.shape[1]
    nb = -(-T // Q_BLOCK)
    kpos = jnp.arange(T)
    gather_rows = jax.vmap(lambda rows, idx: rows[idx])

    def block(args):
        qb, qib, wib, i = args
        qpos = i * Q_BLOCK + jnp.arange(Q_BLOCK)
        causal = kpos[None, :] <= qpos[:, None]
        sc = jnp.where(causal, indexer_scores(qib, k_i, wib), -jnp.inf)
        _, idx = lax.top_k(sc, topk)
        valid = idx <= qpos[None, :, None]
        return sparse_attend(qb, gather_rows(k_d, idx), gather_rows(v_d, idx), valid)

    o = lax.map(block, (to_query_blocks(q_d, nb), to_query_blocks(q_i, nb), to_query_blocks(w_i, nb), jnp.arange(nb)))
    return from_query_blocks(o, T)


def sample_mixers(layer, page_table, cache_mla_latent, cache_mla_kpe, cache_dsa_k, cache_dsa_v, cache_idx_k,
                  q_m, c_kv, k_pe, q_d, k_d, v_d, q_i, k_i, w_i, w_uk, w_uv, g_mla_k, topk):
    T = q_m.shape[1]
    qpos = PAST_LEN + jnp.arange(T)
    kpos = jnp.arange(PAST_LEN + T)
    causal = kpos[None, :] <= qpos[:, None]

    def per_seq(args):
        pt, qm, ckv, kpe, qd, kd, vd, qi, ki, wi = args
        c_all = jnp.concatenate([cache_mla_latent[layer, pt].reshape(PAST_LEN, MLA_KV_LORA), ckv], axis=0)
        p_all = jnp.concatenate([cache_mla_kpe[layer, pt].reshape(PAST_LEN, MLA_ROPE), kpe], axis=0)
        km, vm = mla_keys_values(c_all, p_all, kpos, w_uk, w_uv, g_mla_k)
        s = jnp.einsum('qhd,khd->hqk', qm, km).astype(jnp.float32) * MLA_SCALE
        p = jax.nn.softmax(jnp.where(causal, s, -jnp.inf), axis=-1).astype(vm.dtype)
        o_m = jnp.einsum('hqk,khd->qhd', p, vm).reshape(T, MLA_HEADS * MLA_V)
        ki_all = jnp.concatenate([cache_idx_k[layer, pt].reshape(PAST_LEN, IDX_DIM), ki], axis=0)
        sc = jnp.where(causal, indexer_scores(qi, ki_all, wi), -jnp.inf)
        _, idx = lax.top_k(sc, topk)
        in_past = (idx < PAST_LEN)[..., None, None]
        pidx = jnp.minimum(idx, PAST_LEN - 1)
        phys = pt[pidx // PAGE_SIZE]
        off = pidx % PAGE_SIZE
        nidx = jnp.clip(idx - PAST_LEN, 0, T - 1)
        k_sel = jnp.where(in_past, cache_dsa_k[layer, phys, off], kd[nidx])
        v_sel = jnp.where(in_past, cache_dsa_v[layer, phys, off], vd[nidx])
        o_d = sparse_attend(qd, k_sel, v_sel, idx <= qpos[:, None])
        return o_m, o_d

    return lax.map(per_seq, (page_table, q_m, c_kv, k_pe, q_d, k_d, v_d, q_i, k_i, w_i))


def merge_and_ffn(x, o_m, o_d, gate_a, gate_b, w_branch_a, w_branch_b, w_out, g_ffn, w_gate, w_up, w_down):
    a = jnp.einsum('bte,ed->btd', o_m, w_branch_a)
    b = jnp.einsum('bte,ed->btd', o_d, w_branch_b)
    merged = jax.nn.sigmoid(gate_a) * a + jax.nn.sigmoid(gate_b) * b
    x = x + jnp.einsum('btd,de->bte', merged, w_out)
    h = rmsnorm(x, g_ffn)
    u = jax.nn.silu(jnp.einsum('btd,df->btf', h, w_gate)) * jnp.einsum('btd,df->btf', h, w_up)
    return x + jnp.einsum('btf,fd->btd', u, w_down)


def setup_inputs(seed: int = 0) -> dict:
    key = jax.random.key(seed)
    ks = jax.random.split(key, 32)
    f32 = jnp.float32
    n_pages = PAST_LEN // PAGE_SIZE
    n_used = DEC_BATCH * n_pages
    n_pool = n_used + (n_used + 3) // 4

    def nrm(k, shape, scale):
        return jax.random.normal(k, shape, f32) * scale

    def gain(k, shape):
        return 1.0 + 0.05 * jax.random.normal(k, shape, f32)

    page_table = jax.random.permutation(ks[7], n_pool)[:n_used].reshape(DEC_BATCH, n_pages).astype(jnp.int32)
    return {
        "x_prompt": nrm(ks[0], (BATCH, SEQ, D_MODEL), 1.0),
        "x_sample": nrm(ks[1], (DEC_BATCH, DEC_SEQ, D_MODEL), 1.0),
        "cache_mla_latent": nrm(ks[2], (DEPTH, n_pool, PAGE_SIZE, MLA_KV_LORA), 1.0),
        "cache_mla_kpe": nrm(ks[3], (DEPTH, n_pool, PAGE_SIZE, MLA_ROPE), 1.0),
        "cache_dsa_k": nrm(ks[4], (DEPTH, n_pool, PAGE_SIZE, DSA_KV_HEADS, DSA_HEAD_DIM), 1.0),
        "cache_dsa_v": nrm(ks[5], (DEPTH, n_pool, PAGE_SIZE, DSA_KV_HEADS, DSA_HEAD_DIM), 1.0),
        "cache_idx_k": nrm(ks[6], (DEPTH, n_pool, PAGE_SIZE, IDX_DIM), 1.0),
        "page_table": page_table,
        "meta_tokens": nrm(ks[8], (N_META, D_MODEL), 1.0),
        "g_in": gain(ks[9], (DEPTH, D_MODEL)),
        "w_in": nrm(ks[10], (DEPTH, D_MODEL, IN_WIDTH), D_MODEL ** -0.5),
        "g_q_lat": gain(ks[11], (DEPTH, MLA_Q_LORA)),
        "w_uq": nrm(ks[12], (DEPTH, MLA_Q_LORA, MLA_HEADS, MLA_QK), MLA_Q_LORA ** -0.5),
        "g_mla_q": gain(ks[13], (DEPTH, MLA_QK)),
        "g_kv_lat": gain(ks[14], (DEPTH, MLA_KV_LORA)),
        "w_uk": nrm(ks[15], (DEPTH, MLA_KV_LORA, MLA_HEADS, MLA_NOPE), MLA_KV_LORA ** -0.5),
        "w_uv": nrm(ks[16], (DEPTH, MLA_KV_LORA, MLA_HEADS, MLA_V), MLA_KV_LORA ** -0.5),
        "g_mla_k": gain(ks[17], (DEPTH, MLA_QK)),
        "g_dsa_q": gain(ks[18], (DEPTH, DSA_HEAD_DIM)),
        "g_dsa_k": gain(ks[19], (DEPTH, DSA_HEAD_DIM)),
        "w_branch_a": nrm(ks[20], (DEPTH, MLA_HEADS * MLA_V, D_MODEL), (MLA_HEADS * MLA_V) ** -0.5),
        "w_branch_b": nrm(ks[21], (DEPTH, DSA_HEADS * DSA_HEAD_DIM, D_MODEL), (DSA_HEADS * DSA_HEAD_DIM) ** -0.5),
        "w_out": nrm(ks[22], (DEPTH, D_MODEL, D_MODEL), D_MODEL ** -0.5),
        "g_ffn": gain(ks[23], (DEPTH, D_MODEL)),
        "w_gate": nrm(ks[24], (DEPTH, D_MODEL, D_FF), D_MODEL ** -0.5),
        "w_up": nrm(ks[25], (DEPTH, D_MODEL, D_FF), D_MODEL ** -0.5),
        "w_down": nrm(ks[26], (DEPTH, D_FF, D_MODEL), D_FF ** -0.5),
    }


def reference(x_prompt, x_sample, cache_mla_latent, cache_mla_kpe, cache_dsa_k, cache_dsa_v, cache_idx_k, page_table,
              meta_tokens, g_in, w_in, g_q_lat, w_uq, g_mla_q, g_kv_lat, w_uk, w_uv, g_mla_k, g_dsa_q, g_dsa_k,
              w_branch_a, w_branch_b, w_out, g_ffn, w_gate, w_up, w_down):
    B, S_new = x_prompt.shape[:2]
    L = S_new + N_META
    T = x_sample.shape[1]
    topk_prompt = min(TOPK_MAX, L // 4)
    topk_sample = min(TOPK_MAX, (PAST_LEN + T) // 4)
    pos_prompt = jnp.arange(L)
    pos_sample = PAST_LEN + jnp.arange(T)
    meta = jnp.broadcast_to(meta_tokens.astype(x_prompt.dtype)[None], (B, N_META, D_MODEL))
    xp = jnp.concatenate([meta, x_prompt], axis=1)
    xs = x_sample
    lat_p, kpe_p, dk_p, dv_p, ik_p = [], [], [], [], []
    lat_s, kpe_s, dk_s, dv_s, ik_s = [], [], [], [], []
    for l in range(DEPTH):
        hp = rmsnorm(xp, g_in[l])
        (qm, ckv, kpe, qd, kd, vd, qi, ki, wi, ga, gb) = mixer_inputs(hp, pos_prompt, w_in[l], g_q_lat[l], w_uq[l], g_mla_q[l], g_kv_lat[l], g_dsa_q[l], g_dsa_k[l])
        km, vm = mla_keys_values(ckv, kpe, pos_prompt, w_uk[l], w_uv[l], g_mla_k[l])
        om = mla_prompt_attention(qm, km, vm)
        od = dsa_prompt_attention(qd, kd, vd, qi, ki, wi, topk_prompt)
        xp = merge_and_ffn(xp, om, od, ga, gb, w_branch_a[l], w_branch_b[l], w_out[l], g_ffn[l], w_gate[l], w_up[l], w_down[l])
        lat_p.append(ckv)
        kpe_p.append(kpe)
        dk_p.append(kd)
        dv_p.append(vd)
        ik_p.append(ki)
        hs = rmsnorm(xs, g_in[l])
        (qm, ckv, kpe, qd, kd, vd, qi, ki, wi, ga, gb) = mixer_inputs(hs, pos_sample, w_in[l], g_q_lat[l], w_uq[l], g_mla_q[l], g_kv_lat[l], g_dsa_q[l], g_dsa_k[l])
        om, od = sample_mixers(l, page_table, cache_mla_latent, cache_mla_kpe, cache_dsa_k, cache_dsa_v, cache_idx_k,
                               qm, ckv, kpe, qd, kd, vd, qi, ki, wi, w_uk[l], w_uv[l], g_mla_k[l], topk_sample)
        xs = merge_and_ffn(xs, om, od, ga, gb, w_branch_a[l], w_branch_b[l], w_out[l], g_ffn[l], w_gate[l], w_up[l], w_down[l])
        lat_s.append(ckv)
        kpe_s.append(kpe)
        dk_s.append(kd)
        dv_s.append(vd)
        ik_s.append(ki)
    y_prompt = xp[:, N_META:]
    return (y_prompt, xs,
            jnp.stack(lat_p), jnp.stack(lat_s),
            jnp.stack(kpe_p), jnp.stack(kpe_s),
            jnp.stack(dk_p), jnp.stack(dk_s),
            jnp.stack(dv_p), jnp.stack(dv_s),
            jnp.stack(ik_p), jnp.stack(ik_s))
```

```python
import functools

import jax
import jax.numpy as jnp
from jax import lax
from jax.experimental import pallas as pl
from jax.experimental.pallas import tpu as pltpu

F32 = jnp.float32
BF16 = jnp.bfloat16

N_META = 16
ROPE_THETA = 500000.0
EPS = 1e-6
PAGE_SIZE = 128
MLA_HEADS = 8
MLA_Q_LORA = 384
MLA_KV_LORA = 256
MLA_NOPE = 64
MLA_ROPE = 32
MLA_QK = MLA_NOPE + MLA_ROPE
MLA_V = 64
MLA_SCALE = MLA_QK ** -0.5
DSA_HEADS = 8
DSA_KV_HEADS = 4
DSA_HEAD_DIM = 64
DSA_ROT = DSA_HEAD_DIM // 4
DSA_SCALE = DSA_HEAD_DIM ** -0.5
IDX_HEADS = 8
IDX_DIM = 64
IDX_ROT = IDX_DIM // 4
IDX_SCALE = (IDX_DIM * IDX_HEADS) ** -0.5
TOPK_MAX = 256

LANES = 128
MXU_COLS = 256
HEAD_PAD = LANES
NEG_BIG = -1e30
INT_MIN = -(2 ** 31)
VMEM_LIMIT = 56 * 1024 * 1024

_C_QL = 0
_C_KV = _C_QL + MLA_Q_LORA
_C_QD = _C_KV + MLA_KV_LORA
_C_KD = _C_QD + DSA_HEADS * DSA_HEAD_DIM
_C_VD = _C_KD + DSA_KV_HEADS * DSA_HEAD_DIM
_C_QI = _C_VD + DSA_KV_HEADS * DSA_HEAD_DIM
_C_SM = _C_QI + IDX_HEADS * IDX_DIM
_C_END = _C_SM + LANES
_S_KPE = 0
_S_KI = _S_KPE + MLA_ROPE
_S_WI = _S_KI + IDX_DIM


def _dot(a, b):
    return jnp.dot(a, b, preferred_element_type=F32)


def _dot_nt(a, b):
    return lax.dot_general(a, b, (((1,), (1,)), ((), ())), preferred_element_type=F32)


def _split_bf16(x):
    hi = x.astype(BF16)
    lo = (x - hi.astype(F32)).astype(BF16)
    return hi, lo


def _rms(x, g):
    return x * lax.rsqrt(jnp.mean(x * x, axis=-1, keepdims=True) + EPS) * g


def _head_rms(x, bd, inv_n):
    hi, lo = _split_bf16(x * x)
    parts = []
    for c in range(x.shape[1] // MXU_COLS):
        sl = slice(c * MXU_COLS, (c + 1) * MXU_COLS)
        parts.append(_dot(hi[:, sl], bd) + _dot(lo[:, sl], bd))
    ss = parts[0] if len(parts) == 1 else jnp.concatenate(parts, axis=1)
    return x * lax.rsqrt(ss * inv_n + EPS)


def _rope(x, c, s_lo, s_hi, half):
    w = x.shape[-1]
    return x * c + pltpu.roll(x, w - half, 1) * s_lo + pltpu.roll(x, half, 1) * s_hi


def _proj_kernel(x_ref, gin_ref, wa_ref, gql_ref, wuq_ref, gmq_ref, gkv_ref, wuk_ref, tpe_ref, wuv_ref, gmk_ref,
                 gdq_ref, gdk_ref, bd128_ref, bd64_ref,
                 cm_ref, slm_ref, shm_ref, cd_ref, sld_ref, shd_ref, cs_ref, sls_ref, shs_ref,
                 qm_ref, ckv_ref, km_ref, vm_ref, qd_ref, kd_ref, kdb_ref, vd_ref, vdb_ref, qi_ref, sm_ref):
    hb = _rms(x_ref[0], gin_ref[...]).astype(BF16)

    def proj(a, b):
        return _dot(hb, wa_ref[:, a:b])

    bd128 = bd128_ref[...]
    bd64 = bd64_ref[...]
    cm, slm, shm = cm_ref[...], slm_ref[...], shm_ref[...]
    n_kd = DSA_KV_HEADS * DSA_HEAD_DIM

    ql = _rms(proj(_C_QL, _C_KV), gql_ref[...]).astype(BF16)
    q = _head_rms(_dot(ql, wuq_ref[...]), bd128, 1.0 / MLA_QK) * gmq_ref[...]
    qm_ref[0] = _rope(q, cm, slm, shm, MLA_ROPE // 2).astype(BF16)

    c = _rms(proj(_C_KV, _C_QD), gkv_ref[...])
    ckv_ref[0] = c
    small = proj(_C_SM, _C_END)
    cb = c.astype(BF16)
    pe_hi, pe_lo = _split_bf16(small[:, _S_KPE:_S_KPE + MLA_ROPE])
    k = _dot(cb, wuk_ref[...]) + _dot(pe_hi, tpe_ref[...]) + _dot(pe_lo, tpe_ref[...])
    k = _head_rms(k, bd128, 1.0 / MLA_QK) * gmk_ref[...]
    km_ref[0] = _rope(k, cm, slm, shm, MLA_ROPE // 2).astype(BF16)
    vm_ref[0] = _dot(cb, wuv_ref[...]).astype(BF16)

    qd = _head_rms(proj(_C_QD, _C_KD), bd64, 1.0 / DSA_HEAD_DIM) * gdq_ref[...]
    qd_ref[0] = _rope(qd, cd_ref[...], sld_ref[...], shd_ref[...], DSA_ROT // 2).astype(BF16)
    kd = _head_rms(proj(_C_KD, _C_VD), bd64, 1.0 / DSA_HEAD_DIM) * gdk_ref[...]
    kd = _rope(kd, cd_ref[:, :n_kd], sld_ref[:, :n_kd], shd_ref[:, :n_kd], DSA_ROT // 2)
    kd_ref[0] = kd
    kdb_ref[0] = kd.astype(BF16)
    vd = proj(_C_VD, _C_QI)
    vd_ref[0] = vd
    vdb_ref[0] = vd.astype(BF16)

    qi_ref[0] = _rope(proj(_C_QI, _C_SM), cd_ref[...], sld_ref[...], shd_ref[...], IDX_ROT // 2).astype(BF16)
    sm_ref[0] = _rope(small, cs_ref[...], sls_ref[...], shs_ref[...], IDX_ROT // 2)


def _row_tile(n):
    for t in (272, 256, 128):
        if n % t == 0:
            return t
    raise ValueError(f"unsupported row count {n}")


def _const_spec(a):
    nd = a.ndim
    return pl.BlockSpec(a.shape, lambda *_: (0,) * nd, pipeline_mode=pl.Buffered(1))


def _project(x, wts, tabs):
    B, T, D = x.shape
    tm = _row_tile(T)
    consts = [wts[k] for k in ("g_in", "wa", "g_q_lat", "w_uq", "g_mla_q", "g_kv_lat", "w_uk", "t_pe", "w_uv", "g_mla_k",
                               "g_dsa_q", "g_dsa_k", "bd128", "bd64")]
    tab_list = [tabs[k] for k in ("cm", "slm", "shm", "cd", "sld", "shd", "cs", "sls", "shs")]

    def row_spec(w):
        return pl.BlockSpec((1, tm, w), lambda i, b: (b, i, 0))

    out_w = [(MLA_HEADS * HEAD_PAD, BF16), (MLA_KV_LORA, F32), (MLA_HEADS * HEAD_PAD, BF16), (MLA_HEADS * HEAD_PAD, BF16),
             (DSA_HEADS * DSA_HEAD_DIM, BF16), (DSA_KV_HEADS * DSA_HEAD_DIM, F32), (DSA_KV_HEADS * DSA_HEAD_DIM, BF16),
             (DSA_KV_HEADS * DSA_HEAD_DIM, F32), (DSA_KV_HEADS * DSA_HEAD_DIM, BF16), (IDX_HEADS * IDX_DIM, BF16),
             (LANES, F32)]
    return pl.pallas_call(
        _proj_kernel,
        grid=(T // tm, B),
        in_specs=[row_spec(D)] + [_const_spec(a) for a in consts]
        + [pl.BlockSpec((tm, t.shape[1]), lambda i, b: (i, 0)) for t in tab_list],
        out_specs=[row_spec(w) for w, _ in out_w],
        out_shape=[jax.ShapeDtypeStruct((B, T, w), dt) for w, dt in out_w],
        compiler_params=pltpu.CompilerParams(dimension_semantics=("arbitrary", "arbitrary"), vmem_limit_bytes=VMEM_LIMIT),
        name="proj",
    )(x, *consts, *tab_list)


def _mla_prompt_kernel(q_ref, k_ref, v_ref, o_ref, *, tq, pad_front):
    i = pl.program_id(1)
    qpos = i * tq + lax.broadcasted_iota(jnp.int32, (tq, tq), 0)
    koff = lax.broadcasted_iota(jnp.int32, (tq, tq), 1)
    for h in range(MLA_HEADS):
        hs = slice(h * HEAD_PAD, (h + 1) * HEAD_PAD)
        q = q_ref[0, :, hs]

        def body(j, carry, hs=hs, q=q):
            m, l, acc = carry
            k = k_ref[0, pl.ds(pl.multiple_of(j * tq, tq), tq), hs]
            v = v_ref[0, pl.ds(pl.multiple_of(j * tq, tq), tq), hs]
            kpos = j * tq + koff
            s = jnp.where((kpos <= qpos) & (kpos >= pad_front), _dot_nt(q, k) * MLA_SCALE, NEG_BIG)
            m_new = jnp.maximum(m, jnp.max(s, axis=1, keepdims=True))
            a = jnp.exp(m - m_new)
            p = jnp.exp(s - m_new)
            l = a * l + jnp.sum(p, axis=1, keepdims=True)
            acc = a * acc + _dot(p.astype(BF16), v)
            return m_new, l, acc

        m0 = jnp.full((tq, 1), NEG_BIG, F32)
        l0 = jnp.zeros((tq, 1), F32)
        a0 = jnp.zeros((tq, HEAD_PAD), F32)
        m, l, acc = lax.fori_loop(0, i + 1, body, (m0, l0, a0))
        o_ref[0, :, hs] = (acc / l).astype(BF16)


def _mla_prompt(qm, km, vm, pad_front):
    B, T, W = qm.shape
    tq = LANES
    return pl.pallas_call(
        functools.partial(_mla_prompt_kernel, tq=tq, pad_front=pad_front),
        grid=(B, T // tq),
        in_specs=[pl.BlockSpec((1, tq, W), lambda b, i: (b, i, 0)),
                  pl.BlockSpec((1, T, W), lambda b, i: (b, 0, 0)),
                  pl.BlockSpec((1, T, W), lambda b, i: (b, 0, 0))],
        out_specs=pl.BlockSpec((1, tq, W), lambda b, i: (b, i, 0)),
        out_shape=jax.ShapeDtypeStruct((B, T, W), BF16),
        compiler_params=pltpu.CompilerParams(dimension_semantics=("arbitrary", "arbitrary"), vmem_limit_bytes=VMEM_LIMIT),
        name="mla_prompt",
    )(qm, km, vm)


def _order_keys(scores):
    bits = pltpu.bitcast(scores, jnp.int32)
    keys = bits ^ ((bits >> 31) & 0x7FFFFFFF)
    return jnp.where(bits == INT_MIN, 0, keys)


def _count(mask):
    return jnp.sum(jnp.where(mask, 1.0, 0.0), axis=1, keepdims=True)


def _kth_largest(keys, k):
    kf = jnp.float32(k)
    t0 = jnp.where(_count(keys >= 0) >= kf, 0, INT_MIN).astype(jnp.int32)

    def body(it, t):
        cand = t | (jnp.int32(1) << (30 - it))
        return jnp.where(_count(keys >= cand) >= kf, cand, t)

    return lax.fori_loop(0, 31, body, t0)


def _topk_mask(keys, k, tri_ref):
    n = keys.shape[1]
    t = _kth_largest(keys, k)
    gt = keys > t
    eq = keys == t
    need = jnp.float32(k) - _count(gt)
    tri = tri_ref[...]
    run = jnp.zeros((keys.shape[0], 1), F32)
    parts = []
    for c in range(n // LANES):
        sl = slice(c * LANES, (c + 1) * LANES)
        e = jnp.where(eq[:, sl], 1.0, 0.0)
        pc = _dot(e.astype(BF16), tri) + run
        parts.append(gt[:, sl] | (eq[:, sl] & (pc <= need)))
        run = run + jnp.sum(e, axis=1, keepdims=True)
    return jnp.concatenate(parts, axis=1)


def _dsa_prompt_kernel(qi_ref, wi_ref, ki_ref, qd_ref, kd_ref, vd_ref, tri_ref, o_ref, *, tq, pad_front, topk):
    i = pl.program_id(1)
    T = ki_ref.shape[1]
    qpos = i * tq + lax.broadcasted_iota(jnp.int32, (tq, T), 0)
    kpos = lax.broadcasted_iota(jnp.int32, (tq, T), 1)
    causal = (kpos <= qpos) & (kpos >= pad_front)

    ki = ki_ref[0]
    wi = wi_ref[0]
    sc = jnp.zeros((tq, T), F32)
    for h in range(IDX_HEADS):
        d = _dot_nt(qi_ref[0, :, h * IDX_DIM:(h + 1) * IDX_DIM], ki)
        sc = sc + jnp.maximum(d, 0.0) * wi[:, h:h + 1]
    keys = jnp.where(causal, _order_keys(sc * IDX_SCALE), INT_MIN)
    sel = _topk_mask(keys, topk, tri_ref) & causal

    o_ref[...] = jnp.zeros(o_ref.shape, o_ref.dtype)
    group = DSA_HEADS // DSA_KV_HEADS
    for n in range(DSA_KV_HEADS):
        ns = slice(n * DSA_HEAD_DIM, (n + 1) * DSA_HEAD_DIM)
        k = kd_ref[0, :, ns]
        v = vd_ref[0, :, ns]
        for g in range(group):
            h = n * group + g
            q = qd_ref[0, :, h * DSA_HEAD_DIM:(h + 1) * DSA_HEAD_DIM]
            s = jnp.where(sel, _dot_nt(q, k) * DSA_SCALE, NEG_BIG)
            p = jnp.exp(s - jnp.max(s, axis=1, keepdims=True))
            l = jnp.sum(p, axis=1, keepdims=True)
            o = _dot(p.astype(BF16), v) / l
            o_ref[0, :, h * HEAD_PAD:h * HEAD_PAD + DSA_HEAD_DIM] = o.astype(BF16)


def _dsa_prompt(qi, wi, ki, qd, kd, vd, tri, pad_front, topk):
    B, T, _ = qi.shape
    tq = LANES

    def tile(w):
        return pl.BlockSpec((1, tq, w), lambda b, i: (b, i, 0))

    def full(w):
        return pl.BlockSpec((1, T, w), lambda b, i: (b, 0, 0))

    return pl.pallas_call(
        functools.partial(_dsa_prompt_kernel, tq=tq, pad_front=pad_front, topk=topk),
        grid=(B, T // tq),
        in_specs=[tile(qi.shape[2]), tile(wi.shape[2]), full(ki.shape[2]), tile(qd.shape[2]), full(kd.shape[2]),
                  full(vd.shape[2]), _const_spec(tri)],
        out_specs=tile(DSA_HEADS * HEAD_PAD),
        out_shape=jax.ShapeDtypeStruct((B, T, DSA_HEADS * HEAD_PAD), BF16),
        compiler_params=pltpu.CompilerParams(dimension_semantics=("arbitrary", "arbitrary"), vmem_limit_bytes=VMEM_LIMIT),
        name="dsa_prompt",
    )(qi, wi, ki, qd, kd, vd, tri)


def _merge_ffn_kernel(x_ref, om_ref, od_ref, gin_ref, wg_ref, wba_ref, wbb_ref, wout_ref, gffn_ref, wgate_ref, wup_ref,
                      wdown_ref, y_ref):
    x = x_ref[0]
    d = x.shape[1]
    hb = _rms(x, gin_ref[...]).astype(BF16)
    ga = jax.nn.sigmoid(_dot(hb, wg_ref[:, :d]))
    gb = jax.nn.sigmoid(_dot(hb, wg_ref[:, d:]))
    merged = ga * _dot(om_ref[0], wba_ref[...]) + gb * _dot(od_ref[0], wbb_ref[...])
    x1 = x + _dot(merged.astype(BF16), wout_ref[...])
    h2 = _rms(x1, gffn_ref[...]).astype(BF16)
    u = jax.nn.silu(_dot(h2, wgate_ref[...])) * _dot(h2, wup_ref[...])
    y_ref[0] = x1 + _dot(u.astype(BF16), wdown_ref[...])


def _merge_ffn(x, om, od, wts, row_block_offset, n_rows):
    B, T, D = x.shape
    tm = LANES if T % 256 else 256
    assert n_rows % tm == 0 and (row_block_offset * LANES) % tm == 0
    off = row_block_offset * LANES // tm
    consts = [wts[k] for k in ("g_in", "wg", "w_branch_a", "w_branch_b", "w_out", "g_ffn", "w_gate", "w_up", "w_down")]

    def row_spec(w):
        return pl.BlockSpec((1, tm, w), lambda b, i: (b, i + off, 0))

    return pl.pallas_call(
        _merge_ffn_kernel,
        grid=(B, n_rows // tm),
        in_specs=[row_spec(D), row_spec(om.shape[2]), row_spec(od.shape[2])] + [_const_spec(a) for a in consts],
        out_specs=pl.BlockSpec((1, tm, D), lambda b, i: (b, i, 0)),
        out_shape=jax.ShapeDtypeStruct((B, n_rows, D), F32),
        compiler_params=pltpu.CompilerParams(dimension_semantics=("arbitrary", "arbitrary"), vmem_limit_bytes=VMEM_LIMIT),
        name="merge_ffn",
    )(x, om, od, *consts)


def _absorb_kernel(qm_ref, gk_ref, gsw_ref, wuk_ref, p1_ref, p2_ref, qabs_ref, qq_ref):
    q = qm_ref[...].astype(F32)
    a = q * gk_ref[...]
    b = q * gsw_ref[...]
    lane = lax.broadcasted_iota(jnp.int32, q.shape, 1) % HEAD_PAD
    n_hi, n_lo = _split_bf16(jnp.where(lane >= MLA_ROPE, a, 0.0))
    a_hi, a_lo = _split_bf16(a)
    b_hi, b_lo = _split_bf16(b)
    p1, p2 = p1_ref[...], p2_ref[...]
    for h in range(MLA_HEADS):
        hs = slice(h * HEAD_PAD, (h + 1) * HEAD_PAD)
        w = wuk_ref[:, hs]
        qabs_ref[h] = _dot_nt(n_hi[:, hs], w) + _dot_nt(n_lo[:, hs], w)
        qq_ref[h] = _dot(a_hi[:, hs], p1) + _dot(a_lo[:, hs], p1) + _dot(b_hi[:, hs], p2) + _dot(b_lo[:, hs], p2)


def _absorb(qm, wts):
    n = qm.shape[0]
    return pl.pallas_call(
        _absorb_kernel,
        out_shape=[jax.ShapeDtypeStruct((MLA_HEADS, n, MLA_KV_LORA), F32),
                   jax.ShapeDtypeStruct((MLA_HEADS, n, 2 * MLA_ROPE), F32)],
        name="absorb",
    )(qm, wts["g_mla_k"], wts["g_mla_k_swap"], wts["w_uk"], wts["p_same"], wts["p_cross"])


def _head_rows(full, width):
    row = lax.broadcasted_iota(jnp.int32, (full.shape[0], width), 0)
    out = jnp.zeros((full.shape[0], width), F32)
    for h in range(full.shape[0]):
        out = out + jnp.where(row == h, full[:, h * width:(h + 1) * width], 0.0)
    return out


def _mla_sample_kernel(pt_ref, *refs, n_pages):
    lat = refs[:n_pages]
    kpe = refs[n_pages:2 * n_pages]
    idx = refs[2 * n_pages:3 * n_pages]
    (qabs_ref, qq_ref, qm8_ref, km8_ref, vm8_ref, qi8_ref, wi8_ref, wukt_ref, wuv_ref, eye_ref, cos_ref, sin_ref,
     o_ref, sc_ref, m_ref, l_ref, acc_ref) = refs[3 * n_pages:]
    j = pl.program_id(1)

    @pl.when(j == 0)
    def _():
        m_ref[...] = jnp.full(m_ref.shape, NEG_BIG, F32)
        l_ref[...] = jnp.zeros(l_ref.shape, F32)
        acc_ref[...] = jnp.zeros(acc_ref.shape, F32)

    cb = jnp.concatenate([r[...] for r in lat], axis=0).astype(BF16)
    pe_hi, pe_lo = _split_bf16(jnp.concatenate([r[...] for r in kpe], axis=0))
    kib = jnp.concatenate([r[...] for r in idx], axis=0).astype(BF16)

    kn = _dot_nt(wukt_ref[...], cb)
    ss_nope = jnp.sum((kn * kn).reshape(MLA_HEADS, MLA_NOPE, kn.shape[1]), axis=1)
    pe_t = _dot_nt(eye_ref[...], pe_hi) + _dot_nt(eye_ref[...], pe_lo)
    ss_pe = jnp.sum(pe_t * pe_t, axis=0, keepdims=True)
    feat = jnp.concatenate([pe_t * cos_ref[j], pe_t * sin_ref[j]], axis=0).astype(BF16)
    logit = _dot_nt(qabs_ref[0].astype(BF16), cb) + _dot(qq_ref[0].astype(BF16), feat)
    logit = logit * lax.rsqrt((ss_nope + ss_pe) * (1.0 / MLA_QK) + EPS) * MLA_SCALE

    m_new = jnp.maximum(m_ref[...], jnp.max(logit, axis=1, keepdims=True))
    a = jnp.exp(m_ref[...] - m_new)
    p = jnp.exp(logit - m_new)
    l_ref[...] = a * l_ref[...] + jnp.sum(p, axis=1, keepdims=True)
    acc_ref[...] = a * acc_ref[...] + _dot(p.astype(BF16), cb)
    m_ref[...] = m_new

    d = _dot_nt(qi8_ref[0], kib)
    sc_ref[0] = jnp.sum(jnp.maximum(d, 0.0) * wi8_ref[0], axis=0, keepdims=True) * IDX_SCALE

    @pl.when(j == pl.num_programs(1) - 1)
    def _():
        s_new = jnp.sum(qm8_ref[0].astype(F32) * km8_ref[0].astype(F32), axis=1, keepdims=True) * MLA_SCALE
        m_f = jnp.maximum(m_ref[...], s_new)
        a_f = jnp.exp(m_ref[...] - m_f)
        p_new = jnp.exp(s_new - m_f)
        l_f = a_f * l_ref[...] + p_new
        o_past = _head_rows(_dot((a_f * acc_ref[...]).astype(BF16), wuv_ref[...]), HEAD_PAD)
        o_ref[0] = ((o_past + p_new * vm8_ref[0].astype(F32)) / l_f).astype(BF16)


def _mla_sample(page_table, cache_lat, cache_kpe, cache_idx, qabs, qq, qm8, km8, vm8, qi8, wi8, wts, cos_t, sin_t, n_pages):
    nb, n_tot = page_table.shape
    n_chunks = n_tot // n_pages
    ch = n_pages * PAGE_SIZE

    def page_spec(w, p):
        return pl.BlockSpec((None, None, PAGE_SIZE, w), lambda b, j, pt, p=p: (0, pt[b, j * n_pages + p], 0, 0))

    def seq_spec(a):
        return pl.BlockSpec((1,) + a.shape[1:], lambda b, j, pt: (b, 0, 0))

    def const_spec(a):
        nd = a.ndim
        return pl.BlockSpec(a.shape, lambda b, j, pt: (0,) * nd, pipeline_mode=pl.Buffered(1))

    seq_in = [qabs, qq, qm8, km8, vm8, qi8, wi8]
    consts = [wts["w_uk_t"], wts["w_uv"], wts["eye_pe"], cos_t, sin_t]
    in_specs = ([page_spec(MLA_KV_LORA, p) for p in range(n_pages)] + [page_spec(MLA_ROPE, p) for p in range(n_pages)]
                + [page_spec(IDX_DIM, p) for p in range(n_pages)] + [seq_spec(a) for a in seq_in]
                + [const_spec(a) for a in consts])
    grid_spec = pltpu.PrefetchScalarGridSpec(
        num_scalar_prefetch=1, grid=(nb, n_chunks), in_specs=in_specs,
        out_specs=[pl.BlockSpec((1, MLA_HEADS, HEAD_PAD), lambda b, j, pt: (b, 0, 0)),
                   pl.BlockSpec((1, 1, ch), lambda b, j, pt: (b, 0, j))],
        scratch_shapes=[pltpu.VMEM((MLA_HEADS, 1), F32), pltpu.VMEM((MLA_HEADS, 1), F32),
                        pltpu.VMEM((MLA_HEADS, MLA_KV_LORA), F32)])
    return pl.pallas_call(
        functools.partial(_mla_sample_kernel, n_pages=n_pages),
        grid_spec=grid_spec,
        out_shape=[jax.ShapeDtypeStruct((nb, MLA_HEADS, HEAD_PAD), BF16),
                   jax.ShapeDtypeStruct((nb, 1, n_tot * PAGE_SIZE), F32)],
        compiler_params=pltpu.CompilerParams(dimension_semantics=("arbitrary", "arbitrary"), vmem_limit_bytes=VMEM_LIMIT),
        name="mla_sample",
    )(page_table, *([cache_lat] * n_pages), *([cache_kpe] * n_pages), *([cache_idx] * n_pages), *seq_in, *consts)


def _select_sample_kernel(sc_ref, qi_ref, ki_ref, wi_ref, tri_ref, bias_ref, *, topk):
    n = sc_ref.shape[0]
    ki = ki_ref[...].astype(F32)
    wi = wi_ref[...]
    s_new = jnp.zeros((n, 1), F32)
    for h in range(IDX_HEADS):
        d = jnp.sum(qi_ref[:, h * IDX_DIM:(h + 1) * IDX_DIM].astype(F32) * ki, axis=1, keepdims=True)
        s_new = s_new + jnp.maximum(d, 0.0) * wi[:, h:h + 1]
    lane = lax.broadcasted_iota(jnp.int32, (n, LANES), 1)
    tail = jnp.where(lane == 0, _order_keys(jnp.broadcast_to(s_new * IDX_SCALE, (n, LANES))), INT_MIN)
    keys = jnp.concatenate([_order_keys(sc_ref[...]), tail], axis=1)
    sel = _topk_mask(keys, topk, tri_ref) & (keys > INT_MIN)
    bias_ref[...] = jnp.where(sel, 0.0, NEG_BIG)


def _select_sample(scores, qi, ki, wi, tri, topk):
    n, p = scores.shape
    return pl.pallas_call(
        functools.partial(_select_sample_kernel, topk=topk),
        out_shape=jax.ShapeDtypeStruct((n, p + LANES), F32),
        compiler_params=pltpu.CompilerParams(vmem_limit_bytes=VMEM_LIMIT),
        name="select_sample",
    )(scores, qi, ki, wi, tri)


def _dsa_sample_kernel(pt_ref, *refs, n_pages):
    kp = refs[:n_pages]
    vp = refs[n_pages:2 * n_pages]
    bias_ref, bnew_ref, qd8_ref, kn8_ref, vn8_ref, o_ref, m_ref, l_ref, acc_ref = refs[2 * n_pages:]
    j = pl.program_id(1)
    group = DSA_HEADS // DSA_KV_HEADS

    @pl.when(j == 0)
    def _():
        m_ref[...] = jnp.full(m_ref.shape, NEG_BIG, F32)
        l_ref[...] = jnp.zeros(l_ref.shape, F32)
        acc_ref[...] = jnp.zeros(acc_ref.shape, F32)

    q = qd8_ref[0]
    ch = n_pages * PAGE_SIZE
    kv_head = lax.broadcasted_iota(jnp.int32, (DSA_HEADS, ch), 0) // group
    s = jnp.zeros((DSA_HEADS, ch), F32)
    for n in range(DSA_KV_HEADS):
        k_n = jnp.concatenate([r[:, n, :] for r in kp], axis=0).astype(BF16)
        s = jnp.where(kv_head == n, _dot_nt(q, k_n), s)
    s = s * DSA_SCALE + bias_ref[0]
    m_new = jnp.maximum(m_ref[...], jnp.max(s, axis=1, keepdims=True))
    a = jnp.exp(m_ref[...] - m_new)
    p = jnp.exp(s - m_new)
    l_ref[...] = a * l_ref[...] + jnp.sum(p, axis=1, keepdims=True)
    m_ref[...] = m_new
    pb = p.astype(BF16)
    for n in range(DSA_KV_HEADS):
        v_n = jnp.concatenate([r[:, n, :] for r in vp], axis=0).astype(BF16)
        acc_ref[n] = a * acc_ref[n] + _dot(pb, v_n)

    @pl.when(j == pl.num_programs(1) - 1)
    def _():
        s_new = jnp.sum(q.astype(F32) * kn8_ref[0].astype(BF16).astype(F32), axis=1, keepdims=True) * DSA_SCALE + bnew_ref[0]
        m_f = jnp.maximum(m_ref[...], s_new)
        a_f = jnp.exp(m_ref[...] - m_f)
        p_new = jnp.exp(s_new - m_f)
        l_f = a_f * l_ref[...] + p_new
        row_head = lax.broadcasted_iota(jnp.int32, (DSA_HEADS, DSA_HEAD_DIM), 0) // group
        o = jnp.zeros((DSA_HEADS, DSA_HEAD_DIM), F32)
        for n in range(DSA_KV_HEADS):
            o = o + jnp.where(row_head == n, acc_ref[n], 0.0)
        o_ref[0] = (a_f * o + p_new * vn8_ref[0]) / l_f


def _dsa_sample(page_table, cache_k, cache_v, bias, bias_new, qd8, kn8, vn8, n_pages):
    nb, n_tot = page_table.shape
    n_chunks = n_tot // n_pages
    ch = n_pages * PAGE_SIZE

    def page_spec(p):
        return pl.BlockSpec((None, None, PAGE_SIZE, DSA_KV_HEADS, DSA_HEAD_DIM),
                            lambda b, j, pt, p=p: (0, pt[b, j * n_pages + p], 0, 0, 0))

    def seq_spec(a):
        return pl.BlockSpec((1,) + a.shape[1:], lambda b, j, pt: (b, 0, 0))

    in_specs = ([page_spec(p) for p in range(n_pages)] * 2 + [pl.BlockSpec((1, 1, ch), lambda b, j, pt: (b, 0, j))]
                + [seq_spec(a) for a in (bias_new, qd8, kn8, vn8)])
    grid_spec = pltpu.PrefetchScalarGridSpec(
        num_scalar_prefetch=1, grid=(nb, n_chunks), in_specs=in_specs,
        out_specs=pl.BlockSpec((1, DSA_HEADS, DSA_HEAD_DIM), lambda b, j, pt: (b, 0, 0)),
        scratch_shapes=[pltpu.VMEM((DSA_HEADS, 1), F32), pltpu.VMEM((DSA_HEADS, 1), F32),
                        pltpu.VMEM((DSA_KV_HEADS, DSA_HEADS, DSA_HEAD_DIM), F32)])
    return pl.pallas_call(
        functools.partial(_dsa_sample_kernel, n_pages=n_pages),
        grid_spec=grid_spec,
        out_shape=jax.ShapeDtypeStruct((nb, DSA_HEADS, DSA_HEAD_DIM), F32),
        compiler_params=pltpu.CompilerParams(dimension_semantics=("arbitrary", "arbitrary"), vmem_limit_bytes=VMEM_LIMIT),
        name="dsa_sample",
    )(page_table, *([cache_k] * n_pages), *([cache_v] * n_pages), bias, bias_new, qd8, kn8, vn8)


def _pad_heads(w, lead, n_heads, width, offset=0):
    w = w.reshape(lead + (n_heads, width))
    w = jnp.pad(w, [(0, 0)] * len(lead) + [(0, 0), (offset, HEAD_PAD - width - offset)])
    return w.reshape(lead + (n_heads * HEAD_PAD,))


def _block_diag_ones(width):
    r = jnp.arange(MXU_COLS) // width
    return (r[:, None] == r[None, :]).astype(BF16)


def _prepare_weights(l, g_in, w_in, g_q_lat, w_uq, g_mla_q, g_kv_lat, w_uk, w_uv, g_mla_k, g_dsa_q, g_dsa_k, w_branch_a,
                     w_branch_b, w_out, g_ffn, w_gate, w_up, w_down):
    d = w_in.shape[1]
    splits = (MLA_Q_LORA, MLA_KV_LORA, MLA_ROPE, DSA_HEADS * DSA_HEAD_DIM, DSA_KV_HEADS * DSA_HEAD_DIM,
              DSA_KV_HEADS * DSA_HEAD_DIM, IDX_HEADS * IDX_DIM, IDX_DIM, IDX_HEADS, d, d)
    pts, acc = [], 0
    for s in splits[:-1]:
        acc += s
        pts.append(acc)
    (w_ql, w_kv, w_pe, w_qd, w_kd, w_vd, w_qi, w_ki, w_wi, w_ga, w_gb) = jnp.split(w_in[l], pts, axis=1)
    w_small = jnp.concatenate([w_pe, w_ki, w_wi, jnp.zeros((d, LANES - _S_WI - IDX_HEADS), F32)], axis=1)
    row = lambda g: g.reshape(1, -1).astype(F32)
    eye_pe = jnp.eye(MLA_ROPE, dtype=F32)
    half_pe = MLA_ROPE // 2
    return {
        "g_in": row(g_in[l]),
        "wa": jnp.concatenate([w_ql, w_kv, w_qd, w_kd, w_vd, w_qi, w_small], axis=1).astype(BF16),
        "wg": jnp.concatenate([w_ga, w_gb], axis=1).astype(BF16),
        "g_q_lat": row(g_q_lat[l]),
        "w_uq": _pad_heads(w_uq[l].reshape(MLA_Q_LORA, -1), (MLA_Q_LORA,), MLA_HEADS, MLA_QK).astype(BF16),
        "g_mla_q": row(_pad_heads(jnp.tile(g_mla_q[l], MLA_HEADS), (), MLA_HEADS, MLA_QK)),
        "g_kv_lat": row(g_kv_lat[l]),
        "w_uk": _pad_heads(w_uk[l].reshape(MLA_KV_LORA, -1), (MLA_KV_LORA,), MLA_HEADS, MLA_NOPE, MLA_ROPE).astype(BF16),
        "t_pe": _pad_heads(jnp.tile(eye_pe, (1, MLA_HEADS)), (MLA_ROPE,), MLA_HEADS, MLA_ROPE).astype(BF16),
        "w_uv": _pad_heads(w_uv[l].reshape(MLA_KV_LORA, -1), (MLA_KV_LORA,), MLA_HEADS, MLA_V).astype(BF16),
        "g_mla_k": row(_pad_heads(jnp.tile(g_mla_k[l], MLA_HEADS), (), MLA_HEADS, MLA_QK)),
        "g_dsa_q": row(jnp.tile(g_dsa_q[l], DSA_HEADS)),
        "g_dsa_k": row(jnp.tile(g_dsa_k[l], DSA_KV_HEADS)),
        "g_mla_k_swap": row(_pad_heads(jnp.tile(jnp.concatenate([g_mla_k[l][half_pe:MLA_ROPE], g_mla_k[l][:half_pe]]),
                                                MLA_HEADS), (), MLA_HEADS, MLA_ROPE)),
        "w_uk_t": w_uk[l].reshape(MLA_KV_LORA, -1).T.astype(BF16),
        "eye_pe": eye_pe.astype(BF16),
        "p_same": jnp.pad(eye_pe, ((0, HEAD_PAD - MLA_ROPE), (0, MLA_ROPE))).astype(BF16),
        "p_cross": jnp.pad(jnp.concatenate([jnp.roll(eye_pe, half_pe, axis=0)[:, :half_pe],
                                            -jnp.roll(eye_pe, half_pe, axis=0)[:, half_pe:]], axis=1),
                           ((0, HEAD_PAD - MLA_ROPE), (MLA_ROPE, 0))).astype(BF16),
        "bd128": _block_diag_ones(HEAD_PAD),
        "bd64": _block_diag_ones(DSA_HEAD_DIM),
        "w_branch_a": _pad_heads(w_branch_a[l].T, (d,), MLA_HEADS, MLA_V).T.astype(BF16),
        "w_branch_b": _pad_heads(w_branch_b[l].T, (d,), DSA_HEADS, DSA_HEAD_DIM).T.astype(BF16),
        "w_out": w_out[l].astype(BF16),
        "g_ffn": row(g_ffn[l]),
        "w_gate": w_gate[l].astype(BF16),
        "w_up": w_up[l].astype(BF16),
        "w_down": w_down[l].astype(BF16),
    }


def _head_tables(pos, rot, width):
    half = rot // 2
    inv_freq = ROPE_THETA ** (-jnp.arange(half, dtype=F32) / half)
    ang = pos.astype(F32)[:, None] * inv_freq[None, :]
    cos, sin = jnp.cos(ang), jnp.sin(ang)
    t = pos.shape[0]
    c = jnp.concatenate([cos, cos, jnp.ones((t, width - rot), F32)], axis=1)
    s_lo = jnp.concatenate([-sin, jnp.zeros((t, width - half), F32)], axis=1)
    s_hi = jnp.concatenate([jnp.zeros((t, half), F32), sin, jnp.zeros((t, width - rot), F32)], axis=1)
    return c, s_lo, s_hi


def _rope_tables(pos):
    t = pos.shape[0]
    cm, slm, shm = (jnp.tile(a, (1, MLA_HEADS)) for a in _head_tables(pos, MLA_ROPE, HEAD_PAD))
    cd, sld, shd = (jnp.tile(a, (1, DSA_HEADS)) for a in _head_tables(pos, DSA_ROT, DSA_HEAD_DIM))
    ci, sli, shi = _head_tables(pos, IDX_ROT, IDX_DIM)
    pad = lambda a, fill: jnp.concatenate(
        [jnp.full((t, _S_KI), fill, F32), a, jnp.full((t, LANES - _S_WI), fill, F32)], axis=1)
    return {"cm": cm, "slm": slm, "shm": shm, "cd": cd, "sld": sld, "shd": shd,
            "cs": pad(ci, 1.0), "sls": pad(sli, 0.0), "shs": pad(shi, 0.0)}


def _upper_tri():
    r = jnp.arange(LANES)
    return (r[:, None] <= r[None, :]).astype(BF16)


def _prompt_layer(xp, wts, pad_front, topk, n_out_rows):
    T = xp.shape[1]
    tabs = _rope_tables(jnp.arange(T) - pad_front)
    qm, ckv, km, vm, qd, kd, kdb, vd, vdb, qi, sm = _project(xp, wts, tabs)
    om = _mla_prompt(qm, km, vm, pad_front)
    kib = sm[:, :, _S_KI:_S_WI].astype(BF16)
    wi = sm[:, :, _S_WI:_S_WI + IDX_HEADS]
    od = _dsa_prompt(qi, wi, kib, qd, kdb, vdb, _upper_tri(), pad_front, topk)
    y = _merge_ffn(xp, om, od, wts, (T - n_out_rows) // LANES, n_out_rows)
    return y, ckv, sm, kd, vd


def _past_tables(past, n_pages):
    half = MLA_ROPE // 2
    inv_freq = ROPE_THETA ** (-jnp.arange(half, dtype=F32) / half)
    ang = jnp.arange(past).astype(F32)[:, None] * inv_freq[None, :]
    ch = n_pages * PAGE_SIZE

    def chunked(a):
        a = jnp.concatenate([a, a], axis=1).T.reshape(MLA_ROPE, past // ch, ch)
        return jnp.transpose(a, (1, 0, 2))

    return chunked(jnp.cos(ang)), chunked(jnp.sin(ang))


def _sample_layer(xs, wts, page_table, cache_lat, cache_kpe, cache_dk, cache_dv, cache_ik, topk):
    n, d = xs.shape
    n_tot = page_table.shape[1]
    past = n_tot * PAGE_SIZE
    n_pages = 16 if n_tot % 16 == 0 else n_tot
    x3 = xs.reshape(1, n, d)
    qm, ckv, km, vm, qd, kd, _, vd, _, qi, sm = _project(x3, wts, _rope_tables(jnp.full((n,), past, jnp.int32)))
    qabs, qq = _absorb(qm[0], wts)
    per_head = lambda a, w: a[0].reshape(n, a.shape[2] // w, w)
    ki_new = sm[0, :, _S_KI:_S_WI].astype(BF16)
    wi_new = sm[0, :, _S_WI:_S_WI + IDX_HEADS]
    cos_t, sin_t = _past_tables(past, n_pages)
    om8, scores = _mla_sample(page_table, cache_lat, cache_kpe, cache_ik, jnp.transpose(qabs, (1, 0, 2)),
                              jnp.transpose(qq, (1, 0, 2)), per_head(qm, HEAD_PAD), per_head(km, HEAD_PAD),
                              per_head(vm, HEAD_PAD), per_head(qi, IDX_DIM), wi_new.reshape(n, IDX_HEADS, 1), wts,
                              cos_t, sin_t, n_pages)
    bias = _select_sample(scores.reshape(n, past), qi[0], ki_new, wi_new, _upper_tri(), topk)
    group = DSA_HEADS // DSA_KV_HEADS
    od8 = _dsa_sample(page_table, cache_dk, cache_dv, bias[:, :past].reshape(n, 1, past),
                      bias[:, past:past + 1].reshape(n, 1, 1), per_head(qd, DSA_HEAD_DIM),
                      jnp.repeat(per_head(kd, DSA_HEAD_DIM), group, axis=1),
                      jnp.repeat(per_head(vd, DSA_HEAD_DIM), group, axis=1), n_pages)
    od = jnp.pad(od8, ((0, 0), (0, 0), (0, HEAD_PAD - DSA_HEAD_DIM))).astype(BF16)
    y = _merge_ffn(x3, om8.reshape(1, n, MLA_HEADS * HEAD_PAD), od.reshape(1, n, DSA_HEADS * HEAD_PAD), wts, 0, n)
    return y[0], ckv[0], sm[0], kd[0], vd[0]


def kernel(x_prompt, x_sample, cache_mla_latent, cache_mla_kpe, cache_dsa_k, cache_dsa_v, cache_idx_k, page_table,
           meta_tokens, g_in, w_in, g_q_lat, w_uq, g_mla_q, g_kv_lat, w_uk, w_uv, g_mla_k, g_dsa_q, g_dsa_k, w_branch_a,
           w_branch_b, w_out, g_ffn, w_gate, w_up, w_down):
    B, S, D = x_prompt.shape
    L = S + N_META
    pad_front = (-N_META) % LANES
    topk_prompt = min(TOPK_MAX, L // 4)
    wts = _prepare_weights(0, g_in, w_in, g_q_lat, w_uq, g_mla_q, g_kv_lat, w_uk, w_uv, g_mla_k, g_dsa_q, g_dsa_k,
                           w_branch_a, w_branch_b, w_out, g_ffn, w_gate, w_up, w_down)
    meta = jnp.broadcast_to(meta_tokens.astype(F32)[None], (B, N_META, D))
    xp = jnp.concatenate([jnp.zeros((B, pad_front, D), F32), meta, x_prompt], axis=1)
    y_prompt, ckv, sm, kd, vd = _prompt_layer(xp, wts, pad_front, topk_prompt, S)
    real = slice(pad_front, pad_front + L)

    N, T, _ = x_sample.shape
    assert T == 1 and g_in.shape[0] == 1, "one trunk layer and one new token per running sequence"
    past = page_table.shape[1] * PAGE_SIZE
    topk_sample = min(TOPK_MAX, (past + T) // 4)
    y_s, ckv_s, sm_s, kd_s, vd_s = _sample_layer(x_sample[:, 0], wts, page_table, cache_mla_latent, cache_mla_kpe,
                                                 cache_dsa_k, cache_dsa_v, cache_idx_k, topk_sample)
    heads = (DSA_KV_HEADS, DSA_HEAD_DIM)
    return (y_prompt, y_s[:, None],
            ckv[None, :, real], ckv_s[None, :, None],
            sm[None, :, real, _S_KPE:_S_KI], sm_s[None, :, None, _S_KPE:_S_KI],
            kd[None, :, real].reshape((1, B, L) + heads), kd_s.reshape((1, N, 1) + heads),
            vd[None, :, real].reshape((1, B, L) + heads), vd_s.reshape((1, N, 1) + heads),
            sm[None, :, real, _S_KI:_S_WI], sm_s[None, :, None, _S_KI:_S_WI])
```

```python
import functools

import jax
import jax.numpy as jnp
from jax import lax
from jax.experimental import pallas as pl
from jax.experimental.pallas import tpu as pltpu

F32 = jnp.float32
BF16 = jnp.bfloat16

N_META = 16
ROPE_THETA = 500000.0
EPS = 1e-6
PAGE_SIZE = 128
MLA_HEADS = 8
MLA_Q_LORA = 384
MLA_KV_LORA = 256
MLA_NOPE = 64
MLA_ROPE = 32
MLA_QK = MLA_NOPE + MLA_ROPE
MLA_V = 64
MLA_SCALE = MLA_QK ** -0.5
DSA_HEADS = 8
DSA_KV_HEADS = 4
DSA_HEAD_DIM = 64
DSA_ROT = DSA_HEAD_DIM // 4
DSA_SCALE = DSA_HEAD_DIM ** -0.5
IDX_HEADS = 8
IDX_DIM = 64
IDX_ROT = IDX_DIM // 4
IDX_SCALE = (IDX_DIM * IDX_HEADS) ** -0.5
TOPK_MAX = 256

LANES = 128
MXU_COLS = 256
HEAD_PAD = LANES
NEG_BIG = -1e30
INT_MIN = -(2 ** 31)
VMEM_LIMIT = 56 * 1024 * 1024

_C_QL = 0
_C_KV = _C_QL + MLA_Q_LORA
_C_QD = _C_KV + MLA_KV_LORA
_C_KD = _C_QD + DSA_HEADS * DSA_HEAD_DIM
_C_VD = _C_KD + DSA_KV_HEADS * DSA_HEAD_DIM
_C_QI = _C_VD + DSA_KV_HEADS * DSA_HEAD_DIM
_C_SM = _C_QI + IDX_HEADS * IDX_DIM
_C_END = _C_SM + LANES
_S_KPE = 0
_S_KI = _S_KPE + MLA_ROPE
_S_WI = _S_KI + IDX_DIM


def _dot(a, b):
    return jnp.dot(a, b, preferred_element_type=F32)


def _dot_nt(a, b):
    return lax.dot_general(a, b, (((1,), (1,)), ((), ())), preferred_element_type=F32)


def _split_bf16(x):
    hi = x.astype(BF16)
    lo = (x - hi.astype(F32)).astype(BF16)
    return hi, lo


def _rms(x, g):
    return x * lax.rsqrt(jnp.mean(x * x, axis=-1, keepdims=True) + EPS) * g


def _head_rms(x, bd, inv_n):
    hi, lo = _split_bf16(x * x)
    parts = []
    for c in range(x.shape[1] // MXU_COLS):
        sl = slice(c * MXU_COLS, (c + 1) * MXU_COLS)
        parts.append(_dot(hi[:, sl], bd) + _dot(lo[:, sl], bd))
    ss = parts[0] if len(parts) == 1 else jnp.concatenate(parts, axis=1)
    return x * lax.rsqrt(ss * inv_n + EPS)


def _rope(x, c, s_lo, s_hi, half):
    w = x.shape[-1]
    return x * c + pltpu.roll(x, w - half, 1) * s_lo + pltpu.roll(x, half, 1) * s_hi


def _proj_kernel(x_ref, gin_ref, wa_ref, gql_ref, wuq_ref, gmq_ref, gkv_ref, wuk_ref, tpe_ref, wuv_ref, gmk_ref,
                 gdq_ref, gdk_ref, bd128_ref, bd64_ref,
                 cm_ref, slm_ref, shm_ref, cd_ref, sld_ref, shd_ref, cs_ref, sls_ref, shs_ref,
                 qm_ref, ckv_ref, km_ref, vm_ref, qd_ref, kd_ref, kdb_ref, vd_ref, vdb_ref, qi_ref, sm_ref):
    hb = _rms(x_ref[0], gin_ref[...]).astype(BF16)

    def proj(a, b):
        return _dot(hb, wa_ref[:, a:b])

    bd128 = bd128_ref[...]
    bd64 = bd64_ref[...]
    cm, slm, shm = cm_ref[...], slm_ref[...], shm_ref[...]
    n_kd = DSA_KV_HEADS * DSA_HEAD_DIM

    ql = _rms(proj(_C_QL, _C_KV), gql_ref[...]).astype(BF16)
    q = _head_rms(_dot(ql, wuq_ref[...]), bd128, 1.0 / MLA_QK) * gmq_ref[...]
    qm_ref[0] = (_rope(q, cm, slm, shm, MLA_ROPE // 2) * MLA_SCALE).astype(BF16)

    c = _rms(proj(_C_KV, _C_QD), gkv_ref[...])
    ckv_ref[0] = c
    small = proj(_C_SM, _C_END)
    cb = c.astype(BF16)
    pe_hi, pe_lo = _split_bf16(small[:, _S_KPE:_S_KPE + MLA_ROPE])
    k = _dot(cb, wuk_ref[...]) + _dot(pe_hi, tpe_ref[...]) + _dot(pe_lo, tpe_ref[...])
    k = _head_rms(k, bd128, 1.0 / MLA_QK) * gmk_ref[...]
    km_ref[0] = _rope(k, cm, slm, shm, MLA_ROPE // 2).astype(BF16)
    vm_ref[0] = _dot(cb, wuv_ref[...]).astype(BF16)

    qd = _head_rms(proj(_C_QD, _C_KD), bd64, 1.0 / DSA_HEAD_DIM) * gdq_ref[...]
    qd_ref[0] = (_rope(qd, cd_ref[...], sld_ref[...], shd_ref[...], DSA_ROT // 2) * DSA_SCALE).astype(BF16)
    kd = _head_rms(proj(_C_KD, _C_VD), bd64, 1.0 / DSA_HEAD_DIM) * gdk_ref[...]
    kd = _rope(kd, cd_ref[:, :n_kd], sld_ref[:, :n_kd], shd_ref[:, :n_kd], DSA_ROT // 2)
    kd_ref[0] = kd
    kdb_ref[0] = kd.astype(BF16)
    vd = proj(_C_VD, _C_QI)
    vd_ref[0] = vd
    vdb_ref[0] = vd.astype(BF16)

    qi_ref[0] = _rope(proj(_C_QI, _C_SM), cd_ref[...], sld_ref[...], shd_ref[...], IDX_ROT // 2).astype(BF16)
    sm_ref[0] = _rope(small, cs_ref[...], sls_ref[...], shs_ref[...], IDX_ROT // 2)


def _row_tile(n):
    for t in (256, 128):
        if n % t == 0:
            return t
    raise ValueError(f"unsupported row count {n}")


def _const_spec(a):
    nd = a.ndim
    return pl.BlockSpec(a.shape, lambda *_: (0,) * nd, pipeline_mode=pl.Buffered(1))


def _project(x, wts, tabs):
    B, T, D = x.shape
    tm = _row_tile(T)
    consts = [wts[k] for k in ("g_in", "wa", "g_q_lat", "w_uq", "g_mla_q", "g_kv_lat", "w_uk", "t_pe", "w_uv", "g_mla_k",
                               "g_dsa_q", "g_dsa_k", "bd128", "bd64")]
    tab_list = [tabs[k] for k in ("cm", "slm", "shm", "cd", "sld", "shd", "cs", "sls", "shs")]

    def row_spec(w):
        return pl.BlockSpec((1, tm, w), lambda i, b: (b, i, 0))

    out_w = [(MLA_HEADS * HEAD_PAD, BF16), (MLA_KV_LORA, F32), (MLA_HEADS * HEAD_PAD, BF16), (MLA_HEADS * HEAD_PAD, BF16),
             (DSA_HEADS * DSA_HEAD_DIM, BF16), (DSA_KV_HEADS * DSA_HEAD_DIM, F32), (DSA_KV_HEADS * DSA_HEAD_DIM, BF16),
             (DSA_KV_HEADS * DSA_HEAD_DIM, F32), (DSA_KV_HEADS * DSA_HEAD_DIM, BF16), (IDX_HEADS * IDX_DIM, BF16),
             (LANES, F32)]
    return pl.pallas_call(
        _proj_kernel,
        grid=(T // tm, B),
        in_specs=[row_spec(D)] + [_const_spec(a) for a in consts]
        + [pl.BlockSpec((tm, t.shape[1]), lambda i, b: (i, 0)) for t in tab_list],
        out_specs=[row_spec(w) for w, _ in out_w],
        out_shape=[jax.ShapeDtypeStruct((B, T, w), dt) for w, dt in out_w],
        compiler_params=pltpu.CompilerParams(dimension_semantics=("arbitrary", "arbitrary"), vmem_limit_bytes=VMEM_LIMIT),
        name="proj",
    )(x, *consts, *tab_list)


Q_TILE = MXU_COLS


def _causal_lengths(n_tiles, total):
    return [min(total, (2 * g + 3) * Q_TILE) for g in range((n_tiles + 1) // 2)]


def _causal_valid(i, kv_len, pad_front):
    qpos = (i + 1) * Q_TILE + lax.broadcasted_iota(jnp.int32, (Q_TILE, kv_len), 0)
    kpos = lax.broadcasted_iota(jnp.int32, (Q_TILE, kv_len), 1)
    return (kpos <= qpos) & (kpos >= pad_front)


def _softmax_pv(s, v):
    p = jnp.exp(s - jnp.max(s, axis=1, keepdims=True))
    return _dot(p.astype(BF16), v) / jnp.sum(p, axis=1, keepdims=True)


def _mla_prompt_kernel(q_ref, k_ref, v_ref, o_ref, *, pad_front, lens):
    i = pl.program_id(1)
    for g, kv_len in enumerate(lens):

        @pl.when(i // 2 == g)
        def _(kv_len=kv_len):
            bias = jnp.where(_causal_valid(i, kv_len, pad_front), 0.0, NEG_BIG)
            for h in range(MLA_HEADS):
                hs = slice(h * HEAD_PAD, (h + 1) * HEAD_PAD)
                s = _dot_nt(q_ref[0, :, hs], k_ref[0, :kv_len, hs]) + bias
                o_ref[0, :, hs] = _softmax_pv(s, v_ref[0, :kv_len, hs]).astype(BF16)


def _mla_prompt(qm, km, vm, pad_front, n_rows):
    B, T, W = qm.shape
    n_tiles = n_rows // Q_TILE
    return pl.pallas_call(
        functools.partial(_mla_prompt_kernel, pad_front=pad_front, lens=_causal_lengths(n_tiles, T)),
        grid=(B, n_tiles),
        in_specs=[pl.BlockSpec((1, Q_TILE, W), lambda b, i: (b, i + 1, 0)),
                  pl.BlockSpec((1, T, W), lambda b, i: (b, 0, 0)),
                  pl.BlockSpec((1, T, W), lambda b, i: (b, 0, 0))],
        out_specs=pl.BlockSpec((1, Q_TILE, W), lambda b, i: (b, i, 0)),
        out_shape=jax.ShapeDtypeStruct((B, n_rows, W), BF16),
        compiler_params=pltpu.CompilerParams(dimension_semantics=("arbitrary", "arbitrary"), vmem_limit_bytes=VMEM_LIMIT),
        name="mla_prompt",
    )(qm, km, vm)


KEY_NEG_INF = INT_MIN + 0x7FFFFF


def _key_to_float(key):
    return pltpu.bitcast(jnp.where(key >= 0, key, key ^ 0x7FFFFFFF), F32)


def _count(mask):
    return jnp.sum(jnp.where(mask, 1.0, 0.0), axis=1, keepdims=True)


def _kth_largest(s, k):
    kf = jnp.float32(k)
    t0 = jnp.where(_count(s >= 0.0) >= kf, 0, INT_MIN).astype(jnp.int32)

    def body(it, t):
        cand = t | (jnp.int32(1) << (30 - it))
        ok = (_count(s >= _key_to_float(cand)) >= kf) | (cand <= KEY_NEG_INF)
        return jnp.where(ok, cand, t)

    return _key_to_float(lax.fori_loop(0, 31, body, t0))


def _topk_mask(s, k, tri_ref):
    n = s.shape[1]
    t = _kth_largest(s, k)
    gt = s > t
    eq = s == t
    need = jnp.float32(k) - _count(gt)
    tri = tri_ref[...]
    run = jnp.zeros((s.shape[0], 1), F32)
    parts = []
    for c in range(n // LANES):
        sl = slice(c * LANES, (c + 1) * LANES)
        e = jnp.where(eq[:, sl], 1.0, 0.0)
        pc = _dot(e.astype(BF16), tri) + run
        parts.append(gt[:, sl] | (eq[:, sl] & (pc <= need)))
        run = run + jnp.sum(e, axis=1, keepdims=True)
    return jnp.concatenate(parts, axis=1)


def _dsa_prompt_kernel(qi_ref, wi_ref, ki_ref, qd_ref, kd_ref, vd_ref, tri_ref, o_ref, *, pad_front, topk, lens):
    i = pl.program_id(1)
    group = DSA_HEADS // DSA_KV_HEADS
    o_ref[...] = jnp.zeros(o_ref.shape, o_ref.dtype)
    for g, kv_len in enumerate(lens):

        @pl.when(i // 2 == g)
        def _(kv_len=kv_len):
            valid = _causal_valid(i, kv_len, pad_front)
            ki = ki_ref[0, :kv_len, :]
            wi = wi_ref[0] * IDX_SCALE
            sc = jnp.zeros((Q_TILE, kv_len), F32)
            for h in range(IDX_HEADS):
                d = _dot_nt(qi_ref[0, :, h * IDX_DIM:(h + 1) * IDX_DIM], ki)
                sc = sc + jnp.maximum(d, 0.0) * wi[:, h:h + 1]
            sel = _topk_mask(jnp.where(valid, sc, -jnp.inf), topk, tri_ref) & valid
            bias = jnp.where(sel, 0.0, NEG_BIG)
            for n in range(DSA_KV_HEADS):
                ns = slice(n * DSA_HEAD_DIM, (n + 1) * DSA_HEAD_DIM)
                k = kd_ref[0, :kv_len, ns]
                v = vd_ref[0, :kv_len, ns]
                for j in range(group):
                    h = n * group + j
                    s = _dot_nt(qd_ref[0, :, h * DSA_HEAD_DIM:(h + 1) * DSA_HEAD_DIM], k) + bias
                    o_ref[0, :, h * HEAD_PAD:h * HEAD_PAD + DSA_HEAD_DIM] = _softmax_pv(s, v).astype(BF16)


def _dsa_prompt(qi, wi, ki, qd, kd, vd, tri, pad_front, topk, n_rows):
    B, T, _ = qi.shape
    n_tiles = n_rows // Q_TILE

    def tile(w):
        return pl.BlockSpec((1, Q_TILE, w), lambda b, i: (b, i + 1, 0))

    def full(w):
        return pl.BlockSpec((1, T, w), lambda b, i: (b, 0, 0))

    return pl.pallas_call(
        functools.partial(_dsa_prompt_kernel, pad_front=pad_front, topk=topk, lens=_causal_lengths(n_tiles, T)),
        grid=(B, n_tiles),
        in_specs=[tile(qi.shape[2]), tile(wi.shape[2]), full(ki.shape[2]), tile(qd.shape[2]), full(kd.shape[2]),
                  full(vd.shape[2]), _const_spec(tri)],
        out_specs=pl.BlockSpec((1, Q_TILE, DSA_HEADS * HEAD_PAD), lambda b, i: (b, i, 0)),
        out_shape=jax.ShapeDtypeStruct((B, n_rows, DSA_HEADS * HEAD_PAD), BF16),
        compiler_params=pltpu.CompilerParams(dimension_semantics=("arbitrary", "arbitrary"), vmem_limit_bytes=VMEM_LIMIT),
        name="dsa_prompt",
    )(qi, wi, ki, qd, kd, vd, tri)


def _merge_ffn_kernel(x_ref, om_ref, od_ref, gin_ref, wg_ref, wba_ref, wbb_ref, wout_ref, gffn_ref, wgate_ref, wup_ref,
                      wdown_ref, y_ref):
    x = x_ref[0]
    d = x.shape[1]
    hb = _rms(x, gin_ref[...]).astype(BF16)
    ga = jax.nn.sigmoid(_dot(hb, wg_ref[:, :d]))
    gb = jax.nn.sigmoid(_dot(hb, wg_ref[:, d:]))
    merged = ga * _dot(om_ref[0], wba_ref[...]) + gb * _dot(od_ref[0], wbb_ref[...])
    x1 = x + _dot(merged.astype(BF16), wout_ref[...])
    h2 = _rms(x1, gffn_ref[...]).astype(BF16)
    u = jax.nn.silu(_dot(h2, wgate_ref[...])) * _dot(h2, wup_ref[...])
    y_ref[0] = x1 + _dot(u.astype(BF16), wdown_ref[...])


def _merge_ffn(x, om, od, wts, first_row):
    B, T, D = x.shape
    n_rows = om.shape[1]
    tm = _row_tile(n_rows)
    assert first_row % tm == 0
    off = first_row // tm
    consts = [wts[k] for k in ("g_in", "wg", "w_branch_a", "w_branch_b", "w_out", "g_ffn", "w_gate", "w_up", "w_down")]

    def row_spec(w):
        return pl.BlockSpec((1, tm, w), lambda b, i: (b, i, 0))

    return pl.pallas_call(
        _merge_ffn_kernel,
        grid=(B, n_rows // tm),
        in_specs=[pl.BlockSpec((1, tm, D), lambda b, i: (b, i + off, 0)), row_spec(om.shape[2]), row_spec(od.shape[2])]
        + [_const_spec(a) for a in consts],
        out_specs=row_spec(D),
        out_shape=jax.ShapeDtypeStruct((B, n_rows, D), F32),
        compiler_params=pltpu.CompilerParams(dimension_semantics=("arbitrary", "arbitrary"), vmem_limit_bytes=VMEM_LIMIT),
        name="merge_ffn",
    )(x, om, od, *consts)


def _absorb_kernel(qm_ref, gk_ref, gsw_ref, wuk_ref, p1_ref, p2_ref, qabs_ref, qq_ref):
    q = qm_ref[...].astype(F32)
    a = q * gk_ref[...]
    b = q * gsw_ref[...]
    lane = lax.broadcasted_iota(jnp.int32, q.shape, 1) % HEAD_PAD
    n_hi, n_lo = _split_bf16(jnp.where(lane >= MLA_ROPE, a, 0.0))
    a_hi, a_lo = _split_bf16(a)
    b_hi, b_lo = _split_bf16(b)
    p1, p2 = p1_ref[...], p2_ref[...]
    for h in range(MLA_HEADS):
        hs = slice(h * HEAD_PAD, (h + 1) * HEAD_PAD)
        w = wuk_ref[:, hs]
        qabs_ref[h] = _dot_nt(n_hi[:, hs], w) + _dot_nt(n_lo[:, hs], w)
        qq_ref[h] = _dot(a_hi[:, hs], p1) + _dot(a_lo[:, hs], p1) + _dot(b_hi[:, hs], p2) + _dot(b_lo[:, hs], p2)


def _absorb(qm, wts):
    n = qm.shape[0]
    return pl.pallas_call(
        _absorb_kernel,
        out_shape=[jax.ShapeDtypeStruct((MLA_HEADS, n, MLA_KV_LORA), F32),
                   jax.ShapeDtypeStruct((MLA_HEADS, n, 2 * MLA_ROPE), F32)],
        name="absorb",
    )(qm, wts["g_mla_k"], wts["g_mla_k_swap"], wts["w_uk"], wts["p_same"], wts["p_cross"])


def _head_rows(full, width):
    row = lax.broadcasted_iota(jnp.int32, (full.shape[0], width), 0)
    out = jnp.zeros((full.shape[0], width), F32)
    for h in range(full.shape[0]):
        out = out + jnp.where(row == h, full[:, h * width:(h + 1) * width], 0.0)
    return out


def _mla_sample_kernel(pt_ref, *refs, n_pages):
    lat = refs[:n_pages]
    kpe = refs[n_pages:2 * n_pages]
    idx = refs[2 * n_pages:3 * n_pages]
    (qabs_ref, qq_ref, qm8_ref, km8_ref, vm8_ref, qi8_ref, wi8_ref, wukt_ref, wuv_ref, cos_ref, sin_ref,
     o_ref, sc_ref, m_ref, l_ref, acc_ref) = refs[3 * n_pages:]
    j = pl.program_id(1)

    @pl.when(j == 0)
    def _():
        m_ref[...] = jnp.full(m_ref.shape, NEG_BIG, F32)
        l_ref[...] = jnp.zeros(l_ref.shape, F32)
        acc_ref[...] = jnp.zeros(acc_ref.shape, F32)

    qabs = qabs_ref[0].astype(BF16)
    qq = qq_ref[0].astype(BF16)
    wi = wi8_ref[0] * IDX_SCALE
    half = n_pages // 2
    hw = half * PAGE_SIZE
    parts = []
    for s in range(2):
        pages = slice(s * half, (s + 1) * half)
        cols = slice(s * hw, (s + 1) * hw)
        cb = jnp.concatenate([r[...] for r in lat[pages]], axis=0).astype(BF16)
        pe_t = jnp.concatenate([r[...] for r in kpe[pages]], axis=1)
        ki_t = jnp.concatenate([r[...] for r in idx[pages]], axis=1).astype(BF16)
        kn = _dot_nt(wukt_ref[...], cb)
        ss_nope = jnp.sum((kn * kn).reshape(MLA_HEADS, MLA_NOPE, hw), axis=1)
        ss_pe = jnp.sum(pe_t * pe_t, axis=0, keepdims=True)
        feat = jnp.concatenate([pe_t * cos_ref[j, :, cols], pe_t * sin_ref[j, :, cols]], axis=0).astype(BF16)
        logit = (_dot_nt(qabs, cb) + _dot(qq, feat)) * lax.rsqrt((ss_nope + ss_pe) * (1.0 / MLA_QK) + EPS)
        m_s = jnp.max(logit, axis=1, keepdims=True)
        p = jnp.exp(logit - m_s)
        parts.append((m_s, jnp.sum(p, axis=1, keepdims=True), _dot(p.astype(BF16), cb)))
        d = _dot(qi8_ref[0], ki_t)
        sc_ref[0, :, cols] = jnp.sum(jnp.maximum(d, 0.0) * wi, axis=0, keepdims=True)

    m_old = m_ref[...]
    m_new = jnp.maximum(m_old, jnp.maximum(parts[0][0], parts[1][0]))
    a = jnp.exp(m_old - m_new)
    l_new = a * l_ref[...]
    acc_new = a * acc_ref[...]
    for m_s, l_s, acc_s in parts:
        e = jnp.exp(m_s - m_new)
        l_new = l_new + e * l_s
        acc_new = acc_new + e * acc_s
    m_ref[...] = m_new
    l_ref[...] = l_new
    acc_ref[...] = acc_new

    @pl.when(j == pl.num_programs(1) - 1)
    def _():
        s_new = jnp.sum(qm8_ref[0].astype(F32) * km8_ref[0].astype(F32), axis=1, keepdims=True)
        m_f = jnp.maximum(m_ref[...], s_new)
        a_f = jnp.exp(m_ref[...] - m_f)
        p_new = jnp.exp(s_new - m_f)
        l_f = a_f * l_ref[...] + p_new
        o_past = _head_rows(_dot((a_f * acc_ref[...]).astype(BF16), wuv_ref[...]), HEAD_PAD)
        o_ref[0] = ((o_past + p_new * vm8_ref[0].astype(F32)) / l_f).astype(BF16)


def _mla_sample(page_table, cache_lat, cache_kpe, cache_idx, qabs, qq, qm8, km8, vm8, qi8, wi8, wts, cos_t, sin_t, n_pages):
    nb, n_tot = page_table.shape
    n_chunks = n_tot // n_pages
    ch = n_pages * PAGE_SIZE

    def page_spec(a, p):
        return pl.BlockSpec((None, None) + a.shape[2:], lambda b, j, pt, p=p: (0, pt[b, j * n_pages + p], 0, 0))

    def seq_spec(a):
        return pl.BlockSpec((1,) + a.shape[1:], lambda b, j, pt: (b, 0, 0))

    def const_spec(a):
        nd = a.ndim
        return pl.BlockSpec(a.shape, lambda b, j, pt: (0,) * nd, pipeline_mode=pl.Buffered(1))

    seq_in = [qabs, qq, qm8, km8, vm8, qi8, wi8]
    consts = [wts["w_uk_t"], wts["w_uv"], cos_t, sin_t]
    in_specs = ([page_spec(a, p) for a in (cache_lat, cache_kpe, cache_idx) for p in range(n_pages)]
                + [seq_spec(a) for a in seq_in] + [const_spec(a) for a in consts])
    grid_spec = pltpu.PrefetchScalarGridSpec(
        num_scalar_prefetch=1, grid=(nb, n_chunks), in_specs=in_specs,
        out_specs=[pl.BlockSpec((1, MLA_HEADS, HEAD_PAD), lambda b, j, pt: (b, 0, 0)),
                   pl.BlockSpec((1, 1, ch), lambda b, j, pt: (b, 0, j))],
        scratch_shapes=[pltpu.VMEM((MLA_HEADS, 1), F32), pltpu.VMEM((MLA_HEADS, 1), F32),
                        pltpu.VMEM((MLA_HEADS, MLA_KV_LORA), F32)])
    return pl.pallas_call(
        functools.partial(_mla_sample_kernel, n_pages=n_pages),
        grid_spec=grid_spec,
        out_shape=[jax.ShapeDtypeStruct((nb, MLA_HEADS, HEAD_PAD), BF16),
                   jax.ShapeDtypeStruct((nb, 1, n_tot * PAGE_SIZE), F32)],
        compiler_params=pltpu.CompilerParams(dimension_semantics=("arbitrary", "arbitrary"), vmem_limit_bytes=VMEM_LIMIT),
        name="mla_sample",
    )(page_table, *([cache_lat] * n_pages), *([cache_kpe] * n_pages), *([cache_idx] * n_pages), *seq_in, *consts)


def _select_sample_kernel(sc_ref, qi_ref, ki_ref, wi_ref, tri_ref, bias_ref, *, topk):
    n = sc_ref.shape[0]
    ki = ki_ref[...].astype(F32)
    wi = wi_ref[...] * IDX_SCALE
    s_new = jnp.zeros((n, 1), F32)
    for h in range(IDX_HEADS):
        d = jnp.sum(qi_ref[:, h * IDX_DIM:(h + 1) * IDX_DIM].astype(F32) * ki, axis=1, keepdims=True)
        s_new = s_new + jnp.maximum(d, 0.0) * wi[:, h:h + 1]
    new_lane = lax.broadcasted_iota(jnp.int32, (n, LANES), 1) == 0
    tail = jnp.where(new_lane, s_new, -jnp.inf)
    s = jnp.concatenate([sc_ref[...], tail], axis=1)
    valid = lax.broadcasted_iota(jnp.int32, s.shape, 1) <= sc_ref.shape[1]
    sel = _topk_mask(s, topk, tri_ref) & valid
    bias_ref[...] = jnp.where(sel, 0.0, NEG_BIG)


def _select_sample(scores, qi, ki, wi, tri, topk):
    n, p = scores.shape
    return pl.pallas_call(
        functools.partial(_select_sample_kernel, topk=topk),
        out_shape=jax.ShapeDtypeStruct((n, p + LANES), F32),
        compiler_params=pltpu.CompilerParams(vmem_limit_bytes=VMEM_LIMIT),
        name="select_sample",
    )(scores, qi, ki, wi, tri)


def _dsa_sample_kernel(pt_ref, *refs, n_pages):
    kp = refs[:n_pages]
    vp = refs[n_pages:2 * n_pages]
    bias_ref, bnew_ref, qd8_ref, kn8_ref, vn8_ref, o_ref, m_ref, l_ref, acc_ref = refs[2 * n_pages:]
    j = pl.program_id(1)
    group = DSA_HEADS // DSA_KV_HEADS

    @pl.when(j == 0)
    def _():
        m_ref[...] = jnp.full(m_ref.shape, NEG_BIG, F32)
        l_ref[...] = jnp.zeros(l_ref.shape, F32)
        acc_ref[...] = jnp.zeros(acc_ref.shape, F32)

    q = qd8_ref[0]
    ch = n_pages * PAGE_SIZE
    kv_head = lax.broadcasted_iota(jnp.int32, (DSA_HEADS, ch), 0) // group
    s = jnp.zeros((DSA_HEADS, ch), F32)
    for n in range(DSA_KV_HEADS):
        k_t = jnp.concatenate([r[n] for r in kp], axis=1).astype(BF16)
        s = jnp.where(kv_head == n, _dot(q, k_t), s)
    s = s + bias_ref[0]
    m_new = jnp.maximum(m_ref[...], jnp.max(s, axis=1, keepdims=True))
    a = jnp.exp(m_ref[...] - m_new)
    p = jnp.exp(s - m_new)
    l_ref[...] = a * l_ref[...] + jnp.sum(p, axis=1, keepdims=True)
    m_ref[...] = m_new
    pb = p.astype(BF16)
    for n in range(DSA_KV_HEADS):
        v_t = jnp.concatenate([r[n] for r in vp], axis=1).astype(BF16)
        acc_ref[n] = a * acc_ref[n] + _dot_nt(pb, v_t)

    @pl.when(j == pl.num_programs(1) - 1)
    def _():
        s_new = jnp.sum(q.astype(F32) * kn8_ref[0].astype(BF16).astype(F32), axis=1, keepdims=True) + bnew_ref[0]
        m_f = jnp.maximum(m_ref[...], s_new)
        a_f = jnp.exp(m_ref[...] - m_f)
        p_new = jnp.exp(s_new - m_f)
        l_f = a_f * l_ref[...] + p_new
        row_head = lax.broadcasted_iota(jnp.int32, (DSA_HEADS, DSA_HEAD_DIM), 0) // group
        o = jnp.zeros((DSA_HEADS, DSA_HEAD_DIM), F32)
        for n in range(DSA_KV_HEADS):
            o = o + jnp.where(row_head == n, acc_ref[n], 0.0)
        o_ref[0] = (a_f * o + p_new * vn8_ref[0]) / l_f


def _dsa_sample(page_table, cache_k, cache_v, bias, bias_new, qd8, kn8, vn8, n_pages):
    nb, n_tot = page_table.shape
    n_chunks = n_tot // n_pages
    ch = n_pages * PAGE_SIZE

    def page_spec(p):
        return pl.BlockSpec((None, None, DSA_KV_HEADS, DSA_HEAD_DIM, PAGE_SIZE),
                            lambda b, j, pt, p=p: (0, pt[b, j * n_pages + p], 0, 0, 0))

    def seq_spec(a):
        return pl.BlockSpec((1,) + a.shape[1:], lambda b, j, pt: (b, 0, 0))

    in_specs = ([page_spec(p) for p in range(n_pages)] * 2 + [pl.BlockSpec((1, 1, ch), lambda b, j, pt: (b, 0, j))]
                + [seq_spec(a) for a in (bias_new, qd8, kn8, vn8)])
    grid_spec = pltpu.PrefetchScalarGridSpec(
        num_scalar_prefetch=1, grid=(nb, n_chunks), in_specs=in_specs,
        out_specs=pl.BlockSpec((1, DSA_HEADS, DSA_HEAD_DIM), lambda b, j, pt: (b, 0, 0)),
        scratch_shapes=[pltpu.VMEM((DSA_HEADS, 1), F32), pltpu.VMEM((DSA_HEADS, 1), F32),
                        pltpu.VMEM((DSA_KV_HEADS, DSA_HEADS, DSA_HEAD_DIM), F32)])
    return pl.pallas_call(
        functools.partial(_dsa_sample_kernel, n_pages=n_pages),
        grid_spec=grid_spec,
        out_shape=jax.ShapeDtypeStruct((nb, DSA_HEADS, DSA_HEAD_DIM), F32),
        compiler_params=pltpu.CompilerParams(dimension_semantics=("arbitrary", "arbitrary"), vmem_limit_bytes=VMEM_LIMIT),
        name="dsa_sample",
    )(page_table, *([cache_k] * n_pages), *([cache_v] * n_pages), bias, bias_new, qd8, kn8, vn8)


def _pad_heads(w, lead, n_heads, width, offset=0):
    w = w.reshape(lead + (n_heads, width))
    w = jnp.pad(w, [(0, 0)] * len(lead) + [(0, 0), (offset, HEAD_PAD - width - offset)])
    return w.reshape(lead + (n_heads * HEAD_PAD,))


def _block_diag_ones(width):
    r = jnp.arange(MXU_COLS) // width
    return (r[:, None] == r[None, :]).astype(BF16)


def _prepare_weights(l, g_in, w_in, g_q_lat, w_uq, g_mla_q, g_kv_lat, w_uk, w_uv, g_mla_k, g_dsa_q, g_dsa_k, w_branch_a,
                     w_branch_b, w_out, g_ffn, w_gate, w_up, w_down):
    d = w_in.shape[1]
    splits = (MLA_Q_LORA, MLA_KV_LORA, MLA_ROPE, DSA_HEADS * DSA_HEAD_DIM, DSA_KV_HEADS * DSA_HEAD_DIM,
              DSA_KV_HEADS * DSA_HEAD_DIM, IDX_HEADS * IDX_DIM, IDX_DIM, IDX_HEADS, d, d)
    pts, acc = [], 0
    for s in splits[:-1]:
        acc += s
        pts.append(acc)
    (w_ql, w_kv, w_pe, w_qd, w_kd, w_vd, w_qi, w_ki, w_wi, w_ga, w_gb) = jnp.split(w_in[l], pts, axis=1)
    w_small = jnp.concatenate([w_pe, w_ki, w_wi, jnp.zeros((d, LANES - _S_WI - IDX_HEADS), F32)], axis=1)
    row = lambda g: g.reshape(1, -1).astype(F32)
    eye_pe = jnp.eye(MLA_ROPE, dtype=F32)
    half_pe = MLA_ROPE // 2
    return {
        "g_in": row(g_in[l]),
        "wa": jnp.concatenate([w_ql, w_kv, w_qd, w_kd, w_vd, w_qi, w_small], axis=1).astype(BF16),
        "wg": jnp.concatenate([w_ga, w_gb], axis=1).astype(BF16),
        "g_q_lat": row(g_q_lat[l]),
        "w_uq": _pad_heads(w_uq[l].reshape(MLA_Q_LORA, -1), (MLA_Q_LORA,), MLA_HEADS, MLA_QK).astype(BF16),
        "g_mla_q": row(_pad_heads(jnp.tile(g_mla_q[l], MLA_HEADS), (), MLA_HEADS, MLA_QK)),
        "g_kv_lat": row(g_kv_lat[l]),
        "w_uk": _pad_heads(w_uk[l].reshape(MLA_KV_LORA, -1), (MLA_KV_LORA,), MLA_HEADS, MLA_NOPE, MLA_ROPE).astype(BF16),
        "t_pe": _pad_heads(jnp.tile(eye_pe, (1, MLA_HEADS)), (MLA_ROPE,), MLA_HEADS, MLA_ROPE).astype(BF16),
        "w_uv": _pad_heads(w_uv[l].reshape(MLA_KV_LORA, -1), (MLA_KV_LORA,), MLA_HEADS, MLA_V).astype(BF16),
        "g_mla_k": row(_pad_heads(jnp.tile(g_mla_k[l], MLA_HEADS), (), MLA_HEADS, MLA_QK)),
        "g_dsa_q": row(jnp.tile(g_dsa_q[l], DSA_HEADS)),
        "g_dsa_k": row(jnp.tile(g_dsa_k[l], DSA_KV_HEADS)),
        "g_mla_k_swap": row(_pad_heads(jnp.tile(jnp.concatenate([g_mla_k[l][half_pe:MLA_ROPE], g_mla_k[l][:half_pe]]),
                                                MLA_HEADS), (), MLA_HEADS, MLA_ROPE)),
        "w_uk_t": w_uk[l].reshape(MLA_KV_LORA, -1).T.astype(BF16),
        "eye_pe": eye_pe.astype(BF16),
        "p_same": jnp.pad(eye_pe, ((0, HEAD_PAD - MLA_ROPE), (0, MLA_ROPE))).astype(BF16),
        "p_cross": jnp.pad(jnp.concatenate([jnp.roll(eye_pe, half_pe, axis=0)[:, :half_pe],
                                            -jnp.roll(eye_pe, half_pe, axis=0)[:, half_pe:]], axis=1),
                           ((0, HEAD_PAD - MLA_ROPE), (MLA_ROPE, 0))).astype(BF16),
        "bd128": _block_diag_ones(HEAD_PAD),
        "bd64": _block_diag_ones(DSA_HEAD_DIM),
        "w_branch_a": _pad_heads(w_branch_a[l].T, (d,), MLA_HEADS, MLA_V).T.astype(BF16),
        "w_branch_b": _pad_heads(w_branch_b[l].T, (d,), DSA_HEADS, DSA_HEAD_DIM).T.astype(BF16),
        "w_out": w_out[l].astype(BF16),
        "g_ffn": row(g_ffn[l]),
        "w_gate": w_gate[l].astype(BF16),
        "w_up": w_up[l].astype(BF16),
        "w_down": w_down[l].astype(BF16),
    }


def _head_tables(pos, rot, width):
    half = rot // 2
    inv_freq = ROPE_THETA ** (-jnp.arange(half, dtype=F32) / half)
    ang = pos.astype(F32)[:, None] * inv_freq[None, :]
    cos, sin = jnp.cos(ang), jnp.sin(ang)
    t = pos.shape[0]
    c = jnp.concatenate([cos, cos, jnp.ones((t, width - rot), F32)], axis=1)
    s_lo = jnp.concatenate([-sin, jnp.zeros((t, width - half), F32)], axis=1)
    s_hi = jnp.concatenate([jnp.zeros((t, half), F32), sin, jnp.zeros((t, width - rot), F32)], axis=1)
    return c, s_lo, s_hi


def _rope_tables(pos):
    t = pos.shape[0]
    cm, slm, shm = (jnp.tile(a, (1, MLA_HEADS)) for a in _head_tables(pos, MLA_ROPE, HEAD_PAD))
    cd, sld, shd = (jnp.tile(a, (1, DSA_HEADS)) for a in _head_tables(pos, DSA_ROT, DSA_HEAD_DIM))
    ci, sli, shi = _head_tables(pos, IDX_ROT, IDX_DIM)
    pad = lambda a, fill: jnp.concatenate(
        [jnp.full((t, _S_KI), fill, F32), a, jnp.full((t, LANES - _S_WI), fill, F32)], axis=1)
    return {"cm": cm, "slm": slm, "shm": shm, "cd": cd, "sld": sld, "shd": shd,
            "cs": pad(ci, 1.0), "sls": pad(sli, 0.0), "shs": pad(shi, 0.0)}


def _upper_tri():
    r = jnp.arange(LANES)
    return (r[:, None] <= r[None, :]).astype(BF16)


def _prompt_layer(xp, wts, pad_front, topk, n_out_rows):
    T = xp.shape[1]
    tabs = _rope_tables(jnp.arange(T) - pad_front)
    qm, ckv, km, vm, qd, kd, kdb, vd, vdb, qi, sm = _project(xp, wts, tabs)
    om = _mla_prompt(qm, km, vm, pad_front, n_out_rows)
    kib = sm[:, :, _S_KI:_S_WI].astype(BF16)
    wi = sm[:, :, _S_WI:_S_WI + IDX_HEADS]
    od = _dsa_prompt(qi, wi, kib, qd, kdb, vdb, _upper_tri(), pad_front, topk, n_out_rows)
    y = _merge_ffn(xp, om, od, wts, T - n_out_rows)
    return y, ckv, sm, kd, vd


def _past_tables(past, n_pages):
    half = MLA_ROPE // 2
    inv_freq = ROPE_THETA ** (-jnp.arange(half, dtype=F32) / half)
    ang = jnp.arange(past).astype(F32)[:, None] * inv_freq[None, :]
    ch = n_pages * PAGE_SIZE

    def chunked(a):
        a = jnp.concatenate([a, a], axis=1).T.reshape(MLA_ROPE, past // ch, ch)
        return jnp.transpose(a, (1, 0, 2))

    return chunked(jnp.cos(ang)), chunked(jnp.sin(ang))


def _sample_layer(xs, wts, page_table, cache_lat, cache_kpe, cache_dk, cache_dv, cache_ik, topk):
    n, d = xs.shape
    n_tot = page_table.shape[1]
    past = n_tot * PAGE_SIZE
    n_pages = 32 if n_tot % 32 == 0 else n_tot
    assert n_pages % 2 == 0
    cache_kpe = jnp.swapaxes(cache_kpe, 2, 3)
    cache_ik = jnp.swapaxes(cache_ik, 2, 3)
    cache_dk = jnp.transpose(cache_dk, (0, 1, 3, 4, 2))
    cache_dv = jnp.transpose(cache_dv, (0, 1, 3, 4, 2))
    x3 = xs.reshape(1, n, d)
    qm, ckv, km, vm, qd, kd, _, vd, _, qi, sm = _project(x3, wts, _rope_tables(jnp.full((n,), past, jnp.int32)))
    qabs, qq = _absorb(qm[0], wts)
    per_head = lambda a, w: a[0].reshape(n, a.shape[2] // w, w)
    ki_new = sm[0, :, _S_KI:_S_WI].astype(BF16)
    wi_new = sm[0, :, _S_WI:_S_WI + IDX_HEADS]
    cos_t, sin_t = _past_tables(past, n_pages)
    om8, scores = _mla_sample(page_table, cache_lat, cache_kpe, cache_ik, jnp.transpose(qabs, (1, 0, 2)),
                              jnp.transpose(qq, (1, 0, 2)), per_head(qm, HEAD_PAD), per_head(km, HEAD_PAD),
                              per_head(vm, HEAD_PAD), per_head(qi, IDX_DIM), wi_new.reshape(n, IDX_HEADS, 1), wts,
                              cos_t, sin_t, n_pages)
    bias = _select_sample(scores.reshape(n, past), qi[0], ki_new, wi_new, _upper_tri(), topk)
    group = DSA_HEADS // DSA_KV_HEADS
    od8 = _dsa_sample(page_table, cache_dk, cache_dv, bias[:, :past].reshape(n, 1, past),
                      bias[:, past:past + 1].reshape(n, 1, 1), per_head(qd, DSA_HEAD_DIM),
                      jnp.repeat(per_head(kd, DSA_HEAD_DIM), group, axis=1),
                      jnp.repeat(per_head(vd, DSA_HEAD_DIM), group, axis=1), n_pages)
    od = jnp.pad(od8, ((0, 0), (0, 0), (0, HEAD_PAD - DSA_HEAD_DIM))).astype(BF16)
    y = _merge_ffn(x3, om8.reshape(1, n, MLA_HEADS * HEAD_PAD), od.reshape(1, n, DSA_HEADS * HEAD_PAD), wts, 0)
    return y[0], ckv[0], sm[0], kd[0], vd[0]


def kernel(x_prompt, x_sample, cache_mla_latent, cache_mla_kpe, cache_dsa_k, cache_dsa_v, cache_idx_k, page_table,
           meta_tokens, g_in, w_in, g_q_lat, w_uq, g_mla_q, g_kv_lat, w_uk, w_uv, g_mla_k, g_dsa_q, g_dsa_k, w_branch_a,
           w_branch_b, w_out, g_ffn, w_gate, w_up, w_down):
    B, S, D = x_prompt.shape
    L = S + N_META
    pad_front = (-N_META) % Q_TILE
    topk_prompt = min(TOPK_MAX, L // 4)
    wts = _prepare_weights(0, g_in, w_in, g_q_lat, w_uq, g_mla_q, g_kv_lat, w_uk, w_uv, g_mla_k, g_dsa_q, g_dsa_k,
                           w_branch_a, w_branch_b, w_out, g_ffn, w_gate, w_up, w_down)
    meta = jnp.broadcast_to(meta_tokens.astype(F32)[None], (B, N_META, D))
    xp = jnp.concatenate([jnp.zeros((B, pad_front, D), F32), meta, x_prompt], axis=1)
    y_prompt, ckv, sm, kd, vd = _prompt_layer(xp, wts, pad_front, topk_prompt, S)
    real = slice(pad_front, pad_front + L)

    N, T, _ = x_sample.shape
    assert T == 1 and g_in.shape[0] == 1, "one trunk layer and one new token per running sequence"
    past = page_table.shape[1] * PAGE_SIZE
    topk_sample = min(TOPK_MAX, (past + T) // 4)
    y_s, ckv_s, sm_s, kd_s, vd_s = _sample_layer(x_sample[:, 0], wts, page_table, cache_mla_latent, cache_mla_kpe,
                                                 cache_dsa_k, cache_dsa_v, cache_idx_k, topk_sample)
    heads = (DSA_KV_HEADS, DSA_HEAD_DIM)
    return (y_prompt, y_s[:, None],
            ckv[None, :, real], ckv_s[None, :, None],
            sm[None, :, real, _S_KPE:_S_KI], sm_s[None, :, None, _S_KPE:_S_KI],
            kd[None, :, real].reshape((1, B, L) + heads), kd_s.reshape((1, N, 1) + heads),
            vd[None, :, real].reshape((1, B, L) + heads), vd_s.reshape((1, N, 1) + heads),
            sm[None, :, real, _S_KI:_S_WI], sm_s[None, :, None, _S_KI:_S_WI])
```

```python
import functools

import jax
import jax.numpy as jnp
from jax import lax
from jax.experimental import pallas as pl
from jax.experimental.pallas import tpu as pltpu

F32 = jnp.float32
BF16 = jnp.bfloat16

N_META = 16
ROPE_THETA = 500000.0
EPS = 1e-6
PAGE_SIZE = 128
MLA_HEADS = 8
MLA_Q_LORA = 384
MLA_KV_LORA = 256
MLA_NOPE = 64
MLA_ROPE = 32
MLA_QK = MLA_NOPE + MLA_ROPE
MLA_V = 64
MLA_SCALE = MLA_QK ** -0.5
DSA_HEADS = 8
DSA_KV_HEADS = 4
DSA_HEAD_DIM = 64
DSA_ROT = DSA_HEAD_DIM // 4
DSA_SCALE = DSA_HEAD_DIM ** -0.5
IDX_HEADS = 8
IDX_DIM = 64
IDX_ROT = IDX_DIM // 4
IDX_SCALE = (IDX_DIM * IDX_HEADS) ** -0.5
TOPK_MAX = 256

LANES = 128
MXU_COLS = 256
HEAD_PAD = LANES
NEG_BIG = -1e30
INT_MIN = -(2 ** 31)
VMEM_LIMIT = 56 * 1024 * 1024
PAGES_PER_STEP = 32

_C_QL = 0
_C_KV = _C_QL + MLA_Q_LORA
_C_QD = _C_KV + MLA_KV_LORA
_C_KD = _C_QD + DSA_HEADS * DSA_HEAD_DIM
_C_VD = _C_KD + DSA_KV_HEADS * DSA_HEAD_DIM
_C_QI = _C_VD + DSA_KV_HEADS * DSA_HEAD_DIM
_C_SM = _C_QI + IDX_HEADS * IDX_DIM
_C_END = _C_SM + LANES
_S_KPE = 0
_S_KI = _S_KPE + MLA_ROPE
_S_WI = _S_KI + IDX_DIM


def _dot(a, b):
    return jnp.dot(a, b, preferred_element_type=F32)


def _dot_nt(a, b):
    return lax.dot_general(a, b, (((1,), (1,)), ((), ())), preferred_element_type=F32)


def _split_bf16(x):
    hi = x.astype(BF16)
    lo = (x - hi.astype(F32)).astype(BF16)
    return hi, lo


def _rms(x, g):
    return x * lax.rsqrt(jnp.mean(x * x, axis=-1, keepdims=True) + EPS) * g


def _head_rms(x, bd, inv_n):
    hi, lo = _split_bf16(x * x)
    parts = []
    for c in range(x.shape[1] // MXU_COLS):
        sl = slice(c * MXU_COLS, (c + 1) * MXU_COLS)
        parts.append(_dot(hi[:, sl], bd) + _dot(lo[:, sl], bd))
    ss = parts[0] if len(parts) == 1 else jnp.concatenate(parts, axis=1)
    return x * lax.rsqrt(ss * inv_n + EPS)


def _rope(x, c, s_lo, s_hi, half):
    w = x.shape[-1]
    return x * c + pltpu.roll(x, w - half, 1) * s_lo + pltpu.roll(x, half, 1) * s_hi


def _proj_kernel(x_ref, gin_ref, wa_ref, gql_ref, wuq_ref, gmq_ref, gkv_ref, wuk_ref, tpe_ref, wuv_ref, gmk_ref,
                 gdq_ref, gdk_ref, bd128_ref, bd64_ref,
                 cm_ref, slm_ref, shm_ref, cd_ref, sld_ref, shd_ref, cs_ref, sls_ref, shs_ref,
                 qm_ref, ckv_ref, km_ref, vm_ref, qd_ref, kd_ref, kdb_ref, vd_ref, vdb_ref, qi_ref, sm_ref):
    hb = _rms(x_ref[0], gin_ref[...]).astype(BF16)

    def proj(a, b):
        return _dot(hb, wa_ref[:, a:b])

    bd128 = bd128_ref[...]
    bd64 = bd64_ref[...]
    cm, slm, shm = cm_ref[...], slm_ref[...], shm_ref[...]
    n_kd = DSA_KV_HEADS * DSA_HEAD_DIM

    ql = _rms(proj(_C_QL, _C_KV), gql_ref[...]).astype(BF16)
    q = _head_rms(_dot(ql, wuq_ref[...]), bd128, 1.0 / MLA_QK) * gmq_ref[...]
    qm_ref[0] = (_rope(q, cm, slm, shm, MLA_ROPE // 2) * MLA_SCALE).astype(BF16)

    c = _rms(proj(_C_KV, _C_QD), gkv_ref[...])
    ckv_ref[0] = c
    small = proj(_C_SM, _C_END)
    cb = c.astype(BF16)
    pe_hi, pe_lo = _split_bf16(small[:, _S_KPE:_S_KPE + MLA_ROPE])
    k = _dot(cb, wuk_ref[...]) + _dot(pe_hi, tpe_ref[...]) + _dot(pe_lo, tpe_ref[...])
    k = _head_rms(k, bd128, 1.0 / MLA_QK) * gmk_ref[...]
    km_ref[0] = _rope(k, cm, slm, shm, MLA_ROPE // 2).astype(BF16)
    vm_ref[0] = _dot(cb, wuv_ref[...]).astype(BF16)

    qd = _head_rms(proj(_C_QD, _C_KD), bd64, 1.0 / DSA_HEAD_DIM) * gdq_ref[...]
    qd_ref[0] = (_rope(qd, cd_ref[...], sld_ref[...], shd_ref[...], DSA_ROT // 2) * DSA_SCALE).astype(BF16)
    kd = _head_rms(proj(_C_KD, _C_VD), bd64, 1.0 / DSA_HEAD_DIM) * gdk_ref[...]
    kd = _rope(kd, cd_ref[:, :n_kd], sld_ref[:, :n_kd], shd_ref[:, :n_kd], DSA_ROT // 2)
    kd_ref[0] = kd
    kdb_ref[0] = kd.astype(BF16)
    vd = proj(_C_VD, _C_QI)
    vd_ref[0] = vd
    vdb_ref[0] = vd.astype(BF16)

    qi_ref[0] = _rope(proj(_C_QI, _C_SM), cd_ref[...], sld_ref[...], shd_ref[...], IDX_ROT // 2).astype(BF16)
    sm_ref[0] = _rope(small, cs_ref[...], sls_ref[...], shs_ref[...], IDX_ROT // 2)


def _row_tile(n):
    for t in (256, 128):
        if n % t == 0:
            return t
    raise ValueError(f"unsupported row count {n}")


def _const_spec(a):
    nd = a.ndim
    return pl.BlockSpec(a.shape, lambda *_: (0,) * nd, pipeline_mode=pl.Buffered(1))


def _project(x, wts, tabs):
    B, T, D = x.shape
    tm = _row_tile(T)
    consts = [wts[k] for k in ("g_in", "wa", "g_q_lat", "w_uq", "g_mla_q", "g_kv_lat", "w_uk", "t_pe", "w_uv", "g_mla_k",
                               "g_dsa_q", "g_dsa_k", "bd128", "bd64")]
    tab_list = [tabs[k] for k in ("cm", "slm", "shm", "cd", "sld", "shd", "cs", "sls", "shs")]

    def row_spec(w):
        return pl.BlockSpec((1, tm, w), lambda i, b: (b, i, 0))

    out_w = [(MLA_HEADS * HEAD_PAD, BF16), (MLA_KV_LORA, F32), (MLA_HEADS * HEAD_PAD, BF16), (MLA_HEADS * HEAD_PAD, BF16),
             (DSA_HEADS * DSA_HEAD_DIM, BF16), (DSA_KV_HEADS * DSA_HEAD_DIM, F32), (DSA_KV_HEADS * DSA_HEAD_DIM, BF16),
             (DSA_KV_HEADS * DSA_HEAD_DIM, F32), (DSA_KV_HEADS * DSA_HEAD_DIM, BF16), (IDX_HEADS * IDX_DIM, BF16),
             (LANES, F32)]
    return pl.pallas_call(
        _proj_kernel,
        grid=(T // tm, B),
        in_specs=[row_spec(D)] + [_const_spec(a) for a in consts]
        + [pl.BlockSpec((tm, t.shape[1]), lambda i, b: (i, 0)) for t in tab_list],
        out_specs=[row_spec(w) for w, _ in out_w],
        out_shape=[jax.ShapeDtypeStruct((B, T, w), dt) for w, dt in out_w],
        compiler_params=pltpu.CompilerParams(dimension_semantics=("arbitrary", "arbitrary"), vmem_limit_bytes=VMEM_LIMIT),
        name="proj",
    )(x, *consts, *tab_list)


Q_TILE = MXU_COLS


def _causal_lengths(n_tiles, total):
    return [min(total, (i + 2) * Q_TILE) for i in range(n_tiles)]


def _causal_valid(i, kv_len, pad_front):
    qpos = (i + 1) * Q_TILE + lax.broadcasted_iota(jnp.int32, (Q_TILE, kv_len), 0)
    kpos = lax.broadcasted_iota(jnp.int32, (Q_TILE, kv_len), 1)
    return (kpos <= qpos) & (kpos >= pad_front)


def _softmax_pv(s, v):
    p = jnp.exp(s - jnp.max(s, axis=1, keepdims=True))
    return _dot(p.astype(BF16), v) / jnp.sum(p, axis=1, keepdims=True)


def _mla_prompt_kernel(q_ref, k_ref, v_ref, o_ref, *, pad_front, lens):
    i = pl.program_id(1)
    for g, kv_len in enumerate(lens):

        @pl.when(i == g)
        def _(kv_len=kv_len):
            bias = jnp.where(_causal_valid(i, kv_len, pad_front), 0.0, NEG_BIG)
            for h in range(MLA_HEADS):
                hs = slice(h * HEAD_PAD, (h + 1) * HEAD_PAD)
                s = _dot_nt(q_ref[0, :, hs], k_ref[0, :kv_len, hs]) + bias
                o_ref[0, :, hs] = _softmax_pv(s, v_ref[0, :kv_len, hs]).astype(BF16)


def _mla_prompt(qm, km, vm, pad_front, n_rows):
    B, T, W = qm.shape
    n_tiles = n_rows // Q_TILE
    return pl.pallas_call(
        functools.partial(_mla_prompt_kernel, pad_front=pad_front, lens=_causal_lengths(n_tiles, T)),
        grid=(B, n_tiles),
        in_specs=[pl.BlockSpec((1, Q_TILE, W), lambda b, i: (b, i + 1, 0)),
                  pl.BlockSpec((1, T, W), lambda b, i: (b, 0, 0)),
                  pl.BlockSpec((1, T, W), lambda b, i: (b, 0, 0))],
        out_specs=pl.BlockSpec((1, Q_TILE, W), lambda b, i: (b, i, 0)),
        out_shape=jax.ShapeDtypeStruct((B, n_rows, W), BF16),
        compiler_params=pltpu.CompilerParams(dimension_semantics=("arbitrary", "arbitrary"), vmem_limit_bytes=VMEM_LIMIT),
        name="mla_prompt",
    )(qm, km, vm)


KEY_NEG_INF = INT_MIN + 0x7FFFFF

def _key_to_float(key):
    return pltpu.bitcast(jnp.where(key >= 0, key, key ^ 0x7FFFFFFF), F32)


def _count(mask):
    return jnp.sum(jnp.where(mask, 1.0, 0.0), axis=1, keepdims=True)


def _kth_largest(s, k):
    kf = jnp.float32(k)
    t0 = jnp.where(_count(s >= 0.0) >= kf, 0, INT_MIN).astype(jnp.int32)

    def body(it, t):
        cand = t | (jnp.int32(1) << (30 - it))
        ok = (_count(s >= _key_to_float(cand)) >= kf) | (cand <= KEY_NEG_INF)
        return jnp.where(ok, cand, t)

    return _key_to_float(lax.fori_loop(0, 31, body, t0))


def _topk_mask(s, k, tri_ref):
    n = s.shape[1]
    t = _kth_largest(s, k)
    gt = s > t
    eq = s == t
    need = jnp.float32(k) - _count(gt)
    tri = tri_ref[...]
    run = jnp.zeros((s.shape[0], 1), F32)
    parts = []
    for c in range(n // LANES):
        sl = slice(c * LANES, (c + 1) * LANES)
        e = jnp.where(eq[:, sl], 1.0, 0.0)
        pc = _dot(e.astype(BF16), tri) + run
        parts.append(gt[:, sl] | (eq[:, sl] & (pc <= need)))
        run = run + jnp.sum(e, axis=1, keepdims=True)
    return jnp.concatenate(parts, axis=1)


def _dsa_prompt_kernel(qi_ref, wi_ref, ki_ref, qd_ref, kd_ref, vd_ref, tri_ref, o_ref, *, pad_front, topk, lens):
    i = pl.program_id(1)
    group = DSA_HEADS // DSA_KV_HEADS
    o_ref[...] = jnp.zeros(o_ref.shape, o_ref.dtype)
    for g, kv_len in enumerate(lens):

        @pl.when(i == g)
        def _(kv_len=kv_len):
            valid = _causal_valid(i, kv_len, pad_front)
            ki = ki_ref[0, :kv_len, :]
            wi = wi_ref[0] * IDX_SCALE
            sc = jnp.zeros((Q_TILE, kv_len), F32)
            for h in range(IDX_HEADS):
                d = _dot_nt(qi_ref[0, :, h * IDX_DIM:(h + 1) * IDX_DIM], ki)
                sc = sc + jnp.maximum(d, 0.0) * wi[:, h:h + 1]
            sel = _topk_mask(jnp.where(valid, sc, -jnp.inf), topk, tri_ref) & valid
            bias = jnp.where(sel, 0.0, NEG_BIG)
            for n in range(DSA_KV_HEADS):
                ns = slice(n * DSA_HEAD_DIM, (n + 1) * DSA_HEAD_DIM)
                k = kd_ref[0, :kv_len, ns]
                v = vd_ref[0, :kv_len, ns]
                for j in range(group):
                    h = n * group + j
                    s = _dot_nt(qd_ref[0, :, h * DSA_HEAD_DIM:(h + 1) * DSA_HEAD_DIM], k) + bias
                    o_ref[0, :, h * HEAD_PAD:h * HEAD_PAD + DSA_HEAD_DIM] = _softmax_pv(s, v).astype(BF16)


def _dsa_prompt(qi, wi, ki, qd, kd, vd, tri, pad_front, topk, n_rows):
    B, T, _ = qi.shape
    n_tiles = n_rows // Q_TILE

    def tile(w):
        return pl.BlockSpec((1, Q_TILE, w), lambda b, i: (b, i + 1, 0))

    def full(w):
        return pl.BlockSpec((1, T, w), lambda b, i: (b, 0, 0))

    return pl.pallas_call(
        functools.partial(_dsa_prompt_kernel, pad_front=pad_front, topk=topk, lens=_causal_lengths(n_tiles, T)),
        grid=(B, n_tiles),
        in_specs=[tile(qi.shape[2]), tile(wi.shape[2]), full(ki.shape[2]), tile(qd.shape[2]), full(kd.shape[2]),
                  full(vd.shape[2]), _const_spec(tri)],
        out_specs=pl.BlockSpec((1, Q_TILE, DSA_HEADS * HEAD_PAD), lambda b, i: (b, i, 0)),
        out_shape=jax.ShapeDtypeStruct((B, n_rows, DSA_HEADS * HEAD_PAD), BF16),
        compiler_params=pltpu.CompilerParams(dimension_semantics=("arbitrary", "arbitrary"), vmem_limit_bytes=VMEM_LIMIT),
        name="dsa_prompt",
    )(qi, wi, ki, qd, kd, vd, tri)


def _merge_ffn_kernel(x_ref, om_ref, od_ref, gin_ref, wg_ref, wba_ref, wbb_ref, wout_ref, gffn_ref, wgate_ref, wup_ref,
                      wdown_ref, y_ref):
    x = x_ref[0]
    d = x.shape[1]
    hb = _rms(x, gin_ref[...]).astype(BF16)
    ga = jax.nn.sigmoid(_dot(hb, wg_ref[:, :d]))
    gb = jax.nn.sigmoid(_dot(hb, wg_ref[:, d:]))
    merged = ga * _dot(om_ref[0], wba_ref[...]) + gb * _dot(od_ref[0], wbb_ref[...])
    x1 = x + _dot(merged.astype(BF16), wout_ref[...])
    h2 = _rms(x1, gffn_ref[...]).astype(BF16)
    u = jax.nn.silu(_dot(h2, wgate_ref[...])) * _dot(h2, wup_ref[...])
    y_ref[0] = x1 + _dot(u.astype(BF16), wdown_ref[...])


def _merge_ffn(x, om, od, wts, first_row):
    B, T, D = x.shape
    n_rows = om.shape[1]
    tm = _row_tile(n_rows)
    assert first_row % tm == 0
    off = first_row // tm
    consts = [wts[k] for k in ("g_in", "wg", "w_branch_a", "w_branch_b", "w_out", "g_ffn", "w_gate", "w_up", "w_down")]

    def row_spec(w):
        return pl.BlockSpec((1, tm, w), lambda b, i: (b, i, 0))

    return pl.pallas_call(
        _merge_ffn_kernel,
        grid=(B, n_rows // tm),
        in_specs=[pl.BlockSpec((1, tm, D), lambda b, i: (b, i + off, 0)), row_spec(om.shape[2]), row_spec(od.shape[2])]
        + [_const_spec(a) for a in consts],
        out_specs=row_spec(D),
        out_shape=jax.ShapeDtypeStruct((B, n_rows, D), F32),
        compiler_params=pltpu.CompilerParams(dimension_semantics=("arbitrary", "arbitrary"), vmem_limit_bytes=VMEM_LIMIT),
        name="merge_ffn",
    )(x, om, od, *consts)


def _absorb_kernel(qm_ref, gk_ref, gsw_ref, wuk_ref, p1_ref, p2_ref, qabs_ref, qq_ref):
    q = qm_ref[...].astype(F32)
    a = q * gk_ref[...]
    b = q * gsw_ref[...]
    lane = lax.broadcasted_iota(jnp.int32, q.shape, 1) % HEAD_PAD
    n_hi, n_lo = _split_bf16(jnp.where(lane >= MLA_ROPE, a, 0.0))
    a_hi, a_lo = _split_bf16(a)
    b_hi, b_lo = _split_bf16(b)
    p1, p2 = p1_ref[...], p2_ref[...]
    for h in range(MLA_HEADS):
        hs = slice(h * HEAD_PAD, (h + 1) * HEAD_PAD)
        w = wuk_ref[:, hs]
        qabs_ref[h] = _dot_nt(n_hi[:, hs], w) + _dot_nt(n_lo[:, hs], w)
        qq_ref[h] = _dot(a_hi[:, hs], p1) + _dot(a_lo[:, hs], p1) + _dot(b_hi[:, hs], p2) + _dot(b_lo[:, hs], p2)


def _absorb(qm, wts):
    n = qm.shape[0]
    return pl.pallas_call(
        _absorb_kernel,
        out_shape=[jax.ShapeDtypeStruct((MLA_HEADS, n, MLA_KV_LORA), F32),
                   jax.ShapeDtypeStruct((MLA_HEADS, n, 2 * MLA_ROPE), F32)],
        name="absorb",
    )(qm, wts["g_mla_k"], wts["g_mla_k_swap"], wts["w_uk"], wts["p_same"], wts["p_cross"])


def _head_rows(full, width):
    row = lax.broadcasted_iota(jnp.int32, (full.shape[0], width), 0)
    out = jnp.zeros((full.shape[0], width), F32)
    for h in range(full.shape[0]):
        out = out + jnp.where(row == h, full[:, h * width:(h + 1) * width], 0.0)
    return out


def _page_copies(pt_ref, hbm_refs, bufs, sems, seq, chunk, slot, n_pages, page_of=None):
    copies = []
    for a, (hbm, buf) in enumerate(zip(hbm_refs, bufs)):
        for p in range(n_pages):
            page = pt_ref[seq, chunk * n_pages + p] if page_of is None else page_of
            copies.append(pltpu.make_async_copy(hbm.at[0, page], buf.at[slot, p], sems.at[slot, a]))
    return copies


def _stream_pages(pt_ref, hbm_refs, bufs, sems, n_pages):
    b, j = pl.program_id(0), pl.program_id(1)
    nb, nj = pl.num_programs(0), pl.num_programs(1)
    step = b * nj + j
    slot = step % 2

    @pl.when(step == 0)
    def _():
        for cp in _page_copies(pt_ref, hbm_refs, bufs, sems, b, j, slot, n_pages):
            cp.start()

    @pl.when(step + 1 < nb * nj)
    def _():
        wrap = j + 1 == nj
        for cp in _page_copies(pt_ref, hbm_refs, bufs, sems, jnp.where(wrap, b + 1, b), jnp.where(wrap, 0, j + 1),
                               1 - slot, n_pages):
            cp.start()

    for cp in _page_copies(pt_ref, hbm_refs, bufs, sems, b, j, slot, n_pages, page_of=0):
        cp.wait()
    return slot


def _mla_sample_kernel(pt_ref, lat_hbm, kpe_hbm, idx_hbm, qabs_ref, qq_ref, qm8_ref, km8_ref, vm8_ref, qi8_ref, wi8_ref,
                       wukt_ref, wuv_ref, cos_ref, sin_ref, o_ref, sc_ref, m_ref, l_ref, acc_ref, lat_buf, kpe_buf,
                       idx_buf, sems, *, n_pages):
    j = pl.program_id(1)
    slot = _stream_pages(pt_ref, (lat_hbm, kpe_hbm, idx_hbm), (lat_buf, kpe_buf, idx_buf), sems, n_pages)

    @pl.when(j == 0)
    def _():
        m_ref[...] = jnp.full(m_ref.shape, NEG_BIG, F32)
        l_ref[...] = jnp.zeros(l_ref.shape, F32)
        acc_ref[...] = jnp.zeros(acc_ref.shape, F32)

    n_nope = MLA_HEADS * MLA_NOPE
    lhs = jnp.concatenate([wukt_ref[...], qabs_ref[0].astype(BF16)], axis=0)
    qq = qq_ref[0].astype(BF16)
    wi = wi8_ref[0] * IDX_SCALE
    half = n_pages // 2
    hw = half * PAGE_SIZE
    parts = []
    for s in range(2):
        pages = range(s * half, (s + 1) * half)
        cols = slice(s * hw, (s + 1) * hw)
        cb = lat_buf[slot, s * half:(s + 1) * half].reshape(hw, MLA_KV_LORA).astype(BF16)
        pe_t = jnp.concatenate([kpe_buf[slot, p] for p in pages], axis=1)
        ki_t = jnp.concatenate([idx_buf[slot, p] for p in pages], axis=1).astype(BF16)
        r = _dot_nt(lhs, cb)
        kn = r[:n_nope]
        ss_nope = jnp.sum((kn * kn).reshape(MLA_HEADS, MLA_NOPE, hw), axis=1)
        ss_pe = jnp.sum(pe_t * pe_t, axis=0, keepdims=True)
        feat = jnp.concatenate([pe_t * cos_ref[j, :, cols], pe_t * sin_ref[j, :, cols]], axis=0).astype(BF16)
        logit = (r[n_nope:n_nope + MLA_HEADS] + _dot(qq, feat)) * lax.rsqrt((ss_nope + ss_pe) * (1.0 / MLA_QK) + EPS)
        m_s = jnp.max(logit, axis=1, keepdims=True)
        p = jnp.exp(logit - m_s)
        parts.append((m_s, jnp.sum(p, axis=1, keepdims=True), _dot(p.astype(BF16), cb)))
        d = _dot(qi8_ref[0], ki_t)
        sc_ref[0, :, cols] = jnp.sum(jnp.maximum(d, 0.0) * wi, axis=0, keepdims=True)

    m_old = m_ref[...]
    m_new = jnp.maximum(m_old, jnp.maximum(parts[0][0], parts[1][0]))
    a = jnp.exp(m_old - m_new)
    l_new = a * l_ref[...]
    acc_new = a * acc_ref[...]
    for m_s, l_s, acc_s in parts:
        e = jnp.exp(m_s - m_new)
        l_new = l_new + e * l_s
        acc_new = acc_new + e * acc_s
    m_ref[...] = m_new
    l_ref[...] = l_new
    acc_ref[...] = acc_new

    @pl.when(j == pl.num_programs(1) - 1)
    def _():
        s_new = jnp.sum(qm8_ref[0].astype(F32) * km8_ref[0].astype(F32), axis=1, keepdims=True)
        m_f = jnp.maximum(m_ref[...], s_new)
        a_f = jnp.exp(m_ref[...] - m_f)
        p_new = jnp.exp(s_new - m_f)
        l_f = a_f * l_ref[...] + p_new
        o_past = _head_rows(_dot((a_f * acc_ref[...]).astype(BF16), wuv_ref[...]), HEAD_PAD)
        o_ref[0] = ((o_past + p_new * vm8_ref[0].astype(F32)) / l_f).astype(BF16)


def _mla_sample(page_table, cache_lat, cache_kpe, cache_idx, qabs, qq, qm8, km8, vm8, qi8, wi8, wts, cos_t, sin_t, n_pages):
    nb, n_tot = page_table.shape
    n_chunks = n_tot // n_pages
    ch = n_pages * PAGE_SIZE

    def seq_spec(a):
        return pl.BlockSpec((1,) + a.shape[1:], lambda b, j, pt: (b, 0, 0))

    def const_spec(a):
        nd = a.ndim
        return pl.BlockSpec(a.shape, lambda b, j, pt: (0,) * nd, pipeline_mode=pl.Buffered(1))

    caches = [cache_lat, cache_kpe, cache_idx]
    seq_in = [qabs, qq, qm8, km8, vm8, qi8, wi8]
    consts = [wts["w_uk_t"], wts["w_uv"], cos_t, sin_t]
    in_specs = ([pl.BlockSpec(memory_space=pl.ANY)] * len(caches) + [seq_spec(a) for a in seq_in]
                + [const_spec(a) for a in consts])
    grid_spec = pltpu.PrefetchScalarGridSpec(
        num_scalar_prefetch=1, grid=(nb, n_chunks), in_specs=in_specs,
        out_specs=[pl.BlockSpec((1, MLA_HEADS, HEAD_PAD), lambda b, j, pt: (b, 0, 0)),
                   pl.BlockSpec((1, 1, ch), lambda b, j, pt: (b, 0, j))],
        scratch_shapes=[pltpu.VMEM((MLA_HEADS, 1), F32), pltpu.VMEM((MLA_HEADS, 1), F32),
                        pltpu.VMEM((MLA_HEADS, MLA_KV_LORA), F32)]
        + [pltpu.VMEM((2, n_pages) + a.shape[2:], a.dtype) for a in caches]
        + [pltpu.SemaphoreType.DMA((2, len(caches)))])
    return pl.pallas_call(
        functools.partial(_mla_sample_kernel, n_pages=n_pages),
        grid_spec=grid_spec,
        out_shape=[jax.ShapeDtypeStruct((nb, MLA_HEADS, HEAD_PAD), BF16),
                   jax.ShapeDtypeStruct((nb, 1, n_tot * PAGE_SIZE), F32)],
        compiler_params=pltpu.CompilerParams(dimension_semantics=("arbitrary", "arbitrary"), vmem_limit_bytes=VMEM_LIMIT),
        name="mla_sample",
    )(page_table, *caches, *seq_in, *consts)


def _select_sample_kernel(sc_ref, qi_ref, ki_ref, wi_ref, tri_ref, bias_ref, *, topk):
    n = sc_ref.shape[0]
    ki = ki_ref[...].astype(F32)
    wi = wi_ref[...] * IDX_SCALE
    s_new = jnp.zeros((n, 1), F32)
    for h in range(IDX_HEADS):
        d = jnp.sum(qi_ref[:, h * IDX_DIM:(h + 1) * IDX_DIM].astype(F32) * ki, axis=1, keepdims=True)
        s_new = s_new + jnp.maximum(d, 0.0) * wi[:, h:h + 1]
    new_lane = lax.broadcasted_iota(jnp.int32, (n, LANES), 1) == 0
    tail = jnp.where(new_lane, s_new, -jnp.inf)
    s = jnp.concatenate([sc_ref[...], tail], axis=1)
    valid = lax.broadcasted_iota(jnp.int32, s.shape, 1) <= sc_ref.shape[1]
    sel = _topk_mask(s, topk, tri_ref) & valid
    bias_ref[...] = jnp.where(sel, 0.0, NEG_BIG)


def _select_sample(scores, qi, ki, wi, tri, topk):
    n, p = scores.shape
    return pl.pallas_call(
        functools.partial(_select_sample_kernel, topk=topk),
        out_shape=jax.ShapeDtypeStruct((n, p + LANES), F32),
        compiler_params=pltpu.CompilerParams(vmem_limit_bytes=VMEM_LIMIT),
        name="select_sample",
    )(scores, qi, ki, wi, tri)


def _dsa_sample_kernel(pt_ref, *refs, n_pages):
    kp = refs[:n_pages]
    vp = refs[n_pages:2 * n_pages]
    bias_ref, bnew_ref, qd8_ref, kn8_ref, vn8_ref, o_ref, m_ref, l_ref, acc_ref = refs[2 * n_pages:]
    j = pl.program_id(1)
    group = DSA_HEADS // DSA_KV_HEADS

    @pl.when(j == 0)
    def _():
        m_ref[...] = jnp.full(m_ref.shape, NEG_BIG, F32)
        l_ref[...] = jnp.zeros(l_ref.shape, F32)
        acc_ref[...] = jnp.zeros(acc_ref.shape, F32)

    q = qd8_ref[0]
    ch = n_pages * PAGE_SIZE
    kv_head = lax.broadcasted_iota(jnp.int32, (DSA_HEADS, ch), 0) // group
    s = jnp.zeros((DSA_HEADS, ch), F32)
    for n in range(DSA_KV_HEADS):
        k_t = jnp.concatenate([r[n] for r in kp], axis=1).astype(BF16)
        s = jnp.where(kv_head == n, _dot(q, k_t), s)
    s = s + bias_ref[0]
    m_new = jnp.maximum(m_ref[...], jnp.max(s, axis=1, keepdims=True))
    a = jnp.exp(m_ref[...] - m_new)
    p = jnp.exp(s - m_new)
    l_ref[...] = a * l_ref[...] + jnp.sum(p, axis=1, keepdims=True)
    m_ref[...] = m_new
    pb = p.astype(BF16)
    for n in range(DSA_KV_HEADS):
        v_t = jnp.concatenate([r[n] for r in vp], axis=1).astype(BF16)
        acc_ref[n] = a * acc_ref[n] + _dot_nt(pb, v_t)

    @pl.when(j == pl.num_programs(1) - 1)
    def _():
        s_new = jnp.sum(q.astype(F32) * kn8_ref[0].astype(BF16).astype(F32), axis=1, keepdims=True) + bnew_ref[0]
        m_f = jnp.maximum(m_ref[...], s_new)
        a_f = jnp.exp(m_ref[...] - m_f)
        p_new = jnp.exp(s_new - m_f)
        l_f = a_f * l_ref[...] + p_new
        row_head = lax.broadcasted_iota(jnp.int32, (DSA_HEADS, DSA_HEAD_DIM), 0) // group
        o = jnp.zeros((DSA_HEADS, DSA_HEAD_DIM), F32)
        for n in range(DSA_KV_HEADS):
            o = o + jnp.where(row_head == n, acc_ref[n], 0.0)
        o_ref[0] = (a_f * o + p_new * vn8_ref[0]) / l_f


def _dsa_sample(page_table, cache_k, cache_v, bias, bias_new, qd8, kn8, vn8, n_pages):
    nb, n_tot = page_table.shape
    n_chunks = n_tot // n_pages
    ch = n_pages * PAGE_SIZE

    def page_spec(p):
        return pl.BlockSpec((None, None, DSA_KV_HEADS, DSA_HEAD_DIM, PAGE_SIZE),
                            lambda b, j, pt, p=p: (0, pt[b, j * n_pages + p], 0, 0, 0))

    def seq_spec(a):
        return pl.BlockSpec((1,) + a.shape[1:], lambda b, j, pt: (b, 0, 0))

    in_specs = ([page_spec(p) for p in range(n_pages)] * 2 + [pl.BlockSpec((1, 1, ch), lambda b, j, pt: (b, 0, j))]
                + [seq_spec(a) for a in (bias_new, qd8, kn8, vn8)])
    grid_spec = pltpu.PrefetchScalarGridSpec(
        num_scalar_prefetch=1, grid=(nb, n_chunks), in_specs=in_specs,
        out_specs=pl.BlockSpec((1, DSA_HEADS, DSA_HEAD_DIM), lambda b, j, pt: (b, 0, 0)),
        scratch_shapes=[pltpu.VMEM((DSA_HEADS, 1), F32), pltpu.VMEM((DSA_HEADS, 1), F32),
                        pltpu.VMEM((DSA_KV_HEADS, DSA_HEADS, DSA_HEAD_DIM), F32)])
    return pl.pallas_call(
        functools.partial(_dsa_sample_kernel, n_pages=n_pages),
        grid_spec=grid_spec,
        out_shape=jax.ShapeDtypeStruct((nb, DSA_HEADS, DSA_HEAD_DIM), F32),
        compiler_params=pltpu.CompilerParams(dimension_semantics=("arbitrary", "arbitrary"), vmem_limit_bytes=VMEM_LIMIT),
        name="dsa_sample",
    )(page_table, *([cache_k] * n_pages), *([cache_v] * n_pages), bias, bias_new, qd8, kn8, vn8)


def _pad_heads(w, lead, n_heads, width, offset=0):
    w = w.reshape(lead + (n_heads, width))
    w = jnp.pad(w, [(0, 0)] * len(lead) + [(0, 0), (offset, HEAD_PAD - width - offset)])
    return w.reshape(lead + (n_heads * HEAD_PAD,))


def _block_diag_ones(width):
    r = jnp.arange(MXU_COLS) // width
    return (r[:, None] == r[None, :]).astype(BF16)


def _prepare_weights(l, g_in, w_in, g_q_lat, w_uq, g_mla_q, g_kv_lat, w_uk, w_uv, g_mla_k, g_dsa_q, g_dsa_k, w_branch_a,
                     w_branch_b, w_out, g_ffn, w_gate, w_up, w_down):
    d = w_in.shape[1]
    splits = (MLA_Q_LORA, MLA_KV_LORA, MLA_ROPE, DSA_HEADS * DSA_HEAD_DIM, DSA_KV_HEADS * DSA_HEAD_DIM,
              DSA_KV_HEADS * DSA_HEAD_DIM, IDX_HEADS * IDX_DIM, IDX_DIM, IDX_HEADS, d, d)
    pts, acc = [], 0
    for s in splits[:-1]:
        acc += s
        pts.append(acc)
    (w_ql, w_kv, w_pe, w_qd, w_kd, w_vd, w_qi, w_ki, w_wi, w_ga, w_gb) = jnp.split(w_in[l], pts, axis=1)
    w_small = jnp.concatenate([w_pe, w_ki, w_wi, jnp.zeros((d, LANES - _S_WI - IDX_HEADS), F32)], axis=1)
    row = lambda g: g.reshape(1, -1).astype(F32)
    eye_pe = jnp.eye(MLA_ROPE, dtype=F32)
    half_pe = MLA_ROPE // 2
    return {
        "g_in": row(g_in[l]),
        "wa": jnp.concatenate([w_ql, w_kv, w_qd, w_kd, w_vd, w_qi, w_small], axis=1).astype(BF16),
        "wg": jnp.concatenate([w_ga, w_gb], axis=1).astype(BF16),
        "g_q_lat": row(g_q_lat[l]),
        "w_uq": _pad_heads(w_uq[l].reshape(MLA_Q_LORA, -1), (MLA_Q_LORA,), MLA_HEADS, MLA_QK).astype(BF16),
        "g_mla_q": row(_pad_heads(jnp.tile(g_mla_q[l], MLA_HEADS), (), MLA_HEADS, MLA_QK)),
        "g_kv_lat": row(g_kv_lat[l]),
        "w_uk": _pad_heads(w_uk[l].reshape(MLA_KV_LORA, -1), (MLA_KV_LORA,), MLA_HEADS, MLA_NOPE, MLA_ROPE).astype(BF16),
        "t_pe": _pad_heads(jnp.tile(eye_pe, (1, MLA_HEADS)), (MLA_ROPE,), MLA_HEADS, MLA_ROPE).astype(BF16),
        "w_uv": _pad_heads(w_uv[l].reshape(MLA_KV_LORA, -1), (MLA_KV_LORA,), MLA_HEADS, MLA_V).astype(BF16),
        "g_mla_k": row(_pad_heads(jnp.tile(g_mla_k[l], MLA_HEADS), (), MLA_HEADS, MLA_QK)),
        "g_dsa_q": row(jnp.tile(g_dsa_q[l], DSA_HEADS)),
        "g_dsa_k": row(jnp.tile(g_dsa_k[l], DSA_KV_HEADS)),
        "g_mla_k_swap": row(_pad_heads(jnp.tile(jnp.concatenate([g_mla_k[l][half_pe:MLA_ROPE], g_mla_k[l][:half_pe]]),
                                                MLA_HEADS), (), MLA_HEADS, MLA_ROPE)),
        "w_uk_t": w_uk[l].reshape(MLA_KV_LORA, -1).T.astype(BF16),
        "eye_pe": eye_pe.astype(BF16),
        "p_same": jnp.pad(eye_pe, ((0, HEAD_PAD - MLA_ROPE), (0, MLA_ROPE))).astype(BF16),
        "p_cross": jnp.pad(jnp.concatenate([jnp.roll(eye_pe, half_pe, axis=0)[:, :half_pe],
                                            -jnp.roll(eye_pe, half_pe, axis=0)[:, half_pe:]], axis=1),
                           ((0, HEAD_PAD - MLA_ROPE), (MLA_ROPE, 0))).astype(BF16),
        "bd128": _block_diag_ones(HEAD_PAD),
        "bd64": _block_diag_ones(DSA_HEAD_DIM),
        "w_branch_a": _pad_heads(w_branch_a[l].T, (d,), MLA_HEADS, MLA_V).T.astype(BF16),
        "w_branch_b": _pad_heads(w_branch_b[l].T, (d,), DSA_HEADS, DSA_HEAD_DIM).T.astype(BF16),
        "w_out": w_out[l].astype(BF16),
        "g_ffn": row(g_ffn[l]),
        "w_gate": w_gate[l].astype(BF16),
        "w_up": w_up[l].astype(BF16),
        "w_down": w_down[l].astype(BF16),
    }


def _head_tables(pos, rot, width):
    half = rot // 2
    inv_freq = ROPE_THETA ** (-jnp.arange(half, dtype=F32) / half)
    ang = pos.astype(F32)[:, None] * inv_freq[None, :]
    cos, sin = jnp.cos(ang), jnp.sin(ang)
    t = pos.shape[0]
    c = jnp.concatenate([cos, cos, jnp.ones((t, width - rot), F32)], axis=1)
    s_lo = jnp.concatenate([-sin, jnp.zeros((t, width - half), F32)], axis=1)
    s_hi = jnp.concatenate([jnp.zeros((t, half), F32), sin, jnp.zeros((t, width - rot), F32)], axis=1)
    return c, s_lo, s_hi


def _rope_tables(pos):
    t = pos.shape[0]
    cm, slm, shm = (jnp.tile(a, (1, MLA_HEADS)) for a in _head_tables(pos, MLA_ROPE, HEAD_PAD))
    cd, sld, shd = (jnp.tile(a, (1, DSA_HEADS)) for a in _head_tables(pos, DSA_ROT, DSA_HEAD_DIM))
    ci, sli, shi = _head_tables(pos, IDX_ROT, IDX_DIM)
    pad = lambda a, fill: jnp.concatenate(
        [jnp.full((t, _S_KI), fill, F32), a, jnp.full((t, LANES - _S_WI), fill, F32)], axis=1)
    return {"cm": cm, "slm": slm, "shm": shm, "cd": cd, "sld": sld, "shd": shd,
            "cs": pad(ci, 1.0), "sls": pad(sli, 0.0), "shs": pad(shi, 0.0)}


def _upper_tri():
    r = jnp.arange(LANES)
    return (r[:, None] <= r[None, :]).astype(BF16)


def _prompt_layer(xp, wts, pad_front, topk, n_out_rows):
    T = xp.shape[1]
    tabs = _rope_tables(jnp.arange(T) - pad_front)
    qm, ckv, km, vm, qd, kd, kdb, vd, vdb, qi, sm = _project(xp, wts, tabs)
    om = _mla_prompt(qm, km, vm, pad_front, n_out_rows)
    kib = sm[:, :, _S_KI:_S_WI].astype(BF16)
    wi = sm[:, :, _S_WI:_S_WI + IDX_HEADS]
    od = _dsa_prompt(qi, wi, kib, qd, kdb, vdb, _upper_tri(), pad_front, topk, n_out_rows)
    y = _merge_ffn(xp, om, od, wts, T - n_out_rows)
    return y, ckv, sm, kd, vd


def _past_tables(past, n_pages):
    half = MLA_ROPE // 2
    inv_freq = ROPE_THETA ** (-jnp.arange(half, dtype=F32) / half)
    ang = jnp.arange(past).astype(F32)[:, None] * inv_freq[None, :]
    ch = n_pages * PAGE_SIZE

    def chunked(a):
        a = jnp.concatenate([a, a], axis=1).T.reshape(MLA_ROPE, past // ch, ch)
        return jnp.transpose(a, (1, 0, 2))

    return chunked(jnp.cos(ang)), chunked(jnp.sin(ang))


def _sample_layer(xs, wts, page_table, cache_lat, cache_kpe, cache_dk, cache_dv, cache_ik, topk):
    n, d = xs.shape
    n_tot = page_table.shape[1]
    past = n_tot * PAGE_SIZE
    n_pages = PAGES_PER_STEP if n_tot % PAGES_PER_STEP == 0 else n_tot
    assert n_pages % 2 == 0
    cache_kpe = jnp.swapaxes(cache_kpe, 2, 3)
    cache_ik = jnp.swapaxes(cache_ik, 2, 3)
    cache_dk = jnp.transpose(cache_dk, (0, 1, 3, 4, 2))
    cache_dv = jnp.transpose(cache_dv, (0, 1, 3, 4, 2))
    x3 = xs.reshape(1, n, d)
    qm, ckv, km, vm, qd, kd, _, vd, _, qi, sm = _project(x3, wts, _rope_tables(jnp.full((n,), past, jnp.int32)))
    qabs, qq = _absorb(qm[0], wts)
    per_head = lambda a, w: a[0].reshape(n, a.shape[2] // w, w)
    ki_new = sm[0, :, _S_KI:_S_WI].astype(BF16)
    wi_new = sm[0, :, _S_WI:_S_WI + IDX_HEADS]
    cos_t, sin_t = _past_tables(past, n_pages)
    qabs = jnp.pad(qabs, ((0, 2 * MLA_HEADS - qabs.shape[0]), (0, 0), (0, 0)))
    om8, scores = _mla_sample(page_table, cache_lat, cache_kpe, cache_ik, jnp.transpose(qabs, (1, 0, 2)),
                              jnp.transpose(qq, (1, 0, 2)), per_head(qm, HEAD_PAD), per_head(km, HEAD_PAD),
                              per_head(vm, HEAD_PAD), per_head(qi, IDX_DIM), wi_new.reshape(n, IDX_HEADS, 1), wts,
                              cos_t, sin_t, n_pages)
    bias = _select_sample(scores.reshape(n, past), qi[0], ki_new, wi_new, _upper_tri(), topk)
    group = DSA_HEADS // DSA_KV_HEADS
    od8 = _dsa_sample(page_table, cache_dk, cache_dv, bias[:, :past].reshape(n, 1, past),
                      bias[:, past:past + 1].reshape(n, 1, 1), per_head(qd, DSA_HEAD_DIM),
                      jnp.repeat(per_head(kd, DSA_HEAD_DIM), group, axis=1),
                      jnp.repeat(per_head(vd, DSA_HEAD_DIM), group, axis=1), n_pages)
    od = jnp.pad(od8, ((0, 0), (0, 0), (0, HEAD_PAD - DSA_HEAD_DIM))).astype(BF16)
    y = _merge_ffn(x3, om8.reshape(1, n, MLA_HEADS * HEAD_PAD), od.reshape(1, n, DSA_HEADS * HEAD_PAD), wts, 0)
    return y[0], ckv[0], sm[0], kd[0], vd[0]


def kernel(x_prompt, x_sample, cache_mla_latent, cache_mla_kpe, cache_dsa_k, cache_dsa_v, cache_idx_k, page_table,
           meta_tokens, g_in, w_in, g_q_lat, w_uq, g_mla_q, g_kv_lat, w_uk, w_uv, g_mla_k, g_dsa_q, g_dsa_k, w_branch_a,
           w_branch_b, w_out, g_ffn, w_gate, w_up, w_down):
    B, S, D = x_prompt.shape
    L = S + N_META
    pad_front = (-N_META) % Q_TILE
    topk_prompt = min(TOPK_MAX, L // 4)
    wts = _prepare_weights(0, g_in, w_in, g_q_lat, w_uq, g_mla_q, g_kv_lat, w_uk, w_uv, g_mla_k, g_dsa_q, g_dsa_k,
                           w_branch_a, w_branch_b, w_out, g_ffn, w_gate, w_up, w_down)
    meta = jnp.broadcast_to(meta_tokens.astype(F32)[None], (B, N_META, D))
    xp = jnp.concatenate([jnp.zeros((B, pad_front, D), F32), meta, x_prompt], axis=1)
    y_prompt, ckv, sm, kd, vd = _prompt_layer(xp, wts, pad_front, topk_prompt, S)
    real = slice(pad_front, pad_front + L)

    N, T, _ = x_sample.shape
    assert T == 1 and g_in.shape[0] == 1, "one trunk layer and one new token per running sequence"
    past = page_table.shape[1] * PAGE_SIZE
    topk_sample = min(TOPK_MAX, (past + T) // 4)
    y_s, ckv_s, sm_s, kd_s, vd_s = _sample_layer(x_sample[:, 0], wts, page_table, cache_mla_latent, cache_mla_kpe,
                                                 cache_dsa_k, cache_dsa_v, cache_idx_k, topk_sample)
    heads = (DSA_KV_HEADS, DSA_HEAD_DIM)
    return (y_prompt, y_s[:, None],
            ckv[None, :, real], ckv_s[None, :, None],
            sm[None, :, real, _S_KPE:_S_KI], sm_s[None, :, None, _S_KPE:_S_KI],
            kd[None, :, real].reshape((1, B, L) + heads), kd_s.reshape((1, N, 1) + heads),
            vd[None, :, real].reshape((1, B, L) + heads), vd_s.reshape((1, N, 1) + heads),
            sm[None, :, real, _S_KI:_S_WI], sm_s[None, :, None, _S_KI:_S_WI])
```

```python
import functools

import jax
import jax.numpy as jnp
from jax import lax
from jax.experimental import pallas as pl
from jax.experimental.pallas import tpu as pltpu

F32 = jnp.float32
BF16 = jnp.bfloat16

N_META = 16
ROPE_THETA = 500000.0
EPS = 1e-6
PAGE_SIZE = 128
MLA_HEADS = 8
MLA_Q_LORA = 384
MLA_KV_LORA = 256
MLA_NOPE = 64
MLA_ROPE = 32
MLA_QK = MLA_NOPE + MLA_ROPE
MLA_V = 64
MLA_SCALE = MLA_QK ** -0.5
DSA_HEADS = 8
DSA_KV_HEADS = 4
DSA_HEAD_DIM = 64
DSA_ROT = DSA_HEAD_DIM // 4
DSA_SCALE = DSA_HEAD_DIM ** -0.5
IDX_HEADS = 8
IDX_DIM = 64
IDX_ROT = IDX_DIM // 4
IDX_SCALE = (IDX_DIM * IDX_HEADS) ** -0.5
TOPK_MAX = 256

LANES = 128
MXU_COLS = 256
HEAD_PAD = LANES
NEG_BIG = -1e30
INT_MIN = -(2 ** 31)
VMEM_LIMIT = 56 * 1024 * 1024
PAGES_PER_STEP = 32

_C_QL = 0
_C_KV = _C_QL + MLA_Q_LORA
_C_QD = _C_KV + MLA_KV_LORA
_C_KD = _C_QD + DSA_HEADS * DSA_HEAD_DIM
_C_VD = _C_KD + DSA_KV_HEADS * DSA_HEAD_DIM
_C_QI = _C_VD + DSA_KV_HEADS * DSA_HEAD_DIM
_C_SM = _C_QI + IDX_HEADS * IDX_DIM
_C_END = _C_SM + LANES
_S_KPE = 0
_S_KI = _S_KPE + MLA_ROPE
_S_WI = _S_KI + IDX_DIM


def _dot(a, b):
    return jnp.dot(a, b, preferred_element_type=F32)


def _dot_nt(a, b):
    return lax.dot_general(a, b, (((1,), (1,)), ((), ())), preferred_element_type=F32)


def _split_bf16(x):
    hi = x.astype(BF16)
    lo = (x - hi.astype(F32)).astype(BF16)
    return hi, lo


def _rms(x, g):
    return x * lax.rsqrt(jnp.mean(x * x, axis=-1, keepdims=True) + EPS) * g


def _head_rms(x, bd, inv_n):
    hi, lo = _split_bf16(x * x)
    parts = []
    for c in range(x.shape[1] // MXU_COLS):
        sl = slice(c * MXU_COLS, (c + 1) * MXU_COLS)
        parts.append(_dot(hi[:, sl], bd) + _dot(lo[:, sl], bd))
    ss = parts[0] if len(parts) == 1 else jnp.concatenate(parts, axis=1)
    return x * lax.rsqrt(ss * inv_n + EPS)


def _rope(x, c, s_lo, s_hi, half):
    w = x.shape[-1]
    return x * c + pltpu.roll(x, w - half, 1) * s_lo + pltpu.roll(x, half, 1) * s_hi


def _proj_kernel(x_ref, gin_ref, wa_ref, gql_ref, wuq_ref, gmq_ref, gkv_ref, wuk_ref, tpe_ref, wuv_ref, gmk_ref,
                 gdq_ref, gdk_ref, bd128_ref, bd64_ref,
                 cm_ref, slm_ref, shm_ref, cd_ref, sld_ref, shd_ref, cs_ref, sls_ref, shs_ref,
                 qm_ref, ckv_ref, km_ref, vm_ref, qd_ref, kd_ref, kdb_ref, vd_ref, vdb_ref, qi_ref, sm_ref):
    hb = _rms(x_ref[0], gin_ref[...]).astype(BF16)

    def proj(a, b):
        return _dot(hb, wa_ref[:, a:b])

    bd128 = bd128_ref[...]
    bd64 = bd64_ref[...]
    cm, slm, shm = cm_ref[...], slm_ref[...], shm_ref[...]
    n_kd = DSA_KV_HEADS * DSA_HEAD_DIM

    ql = _rms(proj(_C_QL, _C_KV), gql_ref[...]).astype(BF16)
    q = _head_rms(_dot(ql, wuq_ref[...]), bd128, 1.0 / MLA_QK) * gmq_ref[...]
    qm_ref[0] = (_rope(q, cm, slm, shm, MLA_ROPE // 2) * MLA_SCALE).astype(BF16)

    c = _rms(proj(_C_KV, _C_QD), gkv_ref[...])
    ckv_ref[0] = c
    small = proj(_C_SM, _C_END)
    cb = c.astype(BF16)
    pe_hi, pe_lo = _split_bf16(small[:, _S_KPE:_S_KPE + MLA_ROPE])
    k = _dot(cb, wuk_ref[...]) + _dot(pe_hi, tpe_ref[...]) + _dot(pe_lo, tpe_ref[...])
    k = _head_rms(k, bd128, 1.0 / MLA_QK) * gmk_ref[...]
    km_ref[0] = _rope(k, cm, slm, shm, MLA_ROPE // 2).astype(BF16)
    vm_ref[0] = _dot(cb, wuv_ref[...]).astype(BF16)

    qd = _head_rms(proj(_C_QD, _C_KD), bd64, 1.0 / DSA_HEAD_DIM) * gdq_ref[...]
    qd_ref[0] = (_rope(qd, cd_ref[...], sld_ref[...], shd_ref[...], DSA_ROT // 2) * DSA_SCALE).astype(BF16)
    kd = _head_rms(proj(_C_KD, _C_VD), bd64, 1.0 / DSA_HEAD_DIM) * gdk_ref[...]
    kd = _rope(kd, cd_ref[:, :n_kd], sld_ref[:, :n_kd], shd_ref[:, :n_kd], DSA_ROT // 2)
    kd_ref[0] = kd
    kdb_ref[0] = kd.astype(BF16)
    vd = proj(_C_VD, _C_QI)
    vd_ref[0] = vd
    vdb_ref[0] = vd.astype(BF16)

    qi_ref[0] = _rope(proj(_C_QI, _C_SM), cd_ref[...], sld_ref[...], shd_ref[...], IDX_ROT // 2).astype(BF16)
    sm_ref[0] = _rope(small, cs_ref[...], sls_ref[...], shs_ref[...], IDX_ROT // 2)


def _row_tile(n):
    for t in (256, 128):
        if n % t == 0:
            return t
    raise ValueError(f"unsupported row count {n}")


def _const_spec(a):
    nd = a.ndim
    return pl.BlockSpec(a.shape, lambda *_: (0,) * nd, pipeline_mode=pl.Buffered(1))


def _project(x, wts, tabs):
    B, T, D = x.shape
    tm = _row_tile(T)
    consts = [wts[k] for k in ("g_in", "wa", "g_q_lat", "w_uq", "g_mla_q", "g_kv_lat", "w_uk", "t_pe", "w_uv", "g_mla_k",
                               "g_dsa_q", "g_dsa_k", "bd128", "bd64")]
    tab_list = [tabs[k] for k in ("cm", "slm", "shm", "cd", "sld", "shd", "cs", "sls", "shs")]

    def row_spec(w):
        return pl.BlockSpec((1, tm, w), lambda i, b: (b, i, 0))

    out_w = [(MLA_HEADS * HEAD_PAD, BF16), (MLA_KV_LORA, F32), (MLA_HEADS * HEAD_PAD, BF16), (MLA_HEADS * HEAD_PAD, BF16),
             (DSA_HEADS * DSA_HEAD_DIM, BF16), (DSA_KV_HEADS * DSA_HEAD_DIM, F32), (DSA_KV_HEADS * DSA_HEAD_DIM, BF16),
             (DSA_KV_HEADS * DSA_HEAD_DIM, F32), (DSA_KV_HEADS * DSA_HEAD_DIM, BF16), (IDX_HEADS * IDX_DIM, BF16),
             (LANES, F32)]
    return pl.pallas_call(
        _proj_kernel,
        grid=(T // tm, B),
        in_specs=[row_spec(D)] + [_const_spec(a) for a in consts]
        + [pl.BlockSpec((tm, t.shape[1]), lambda i, b: (i, 0)) for t in tab_list],
        out_specs=[row_spec(w) for w, _ in out_w],
        out_shape=[jax.ShapeDtypeStruct((B, T, w), dt) for w, dt in out_w],
        compiler_params=pltpu.CompilerParams(dimension_semantics=("arbitrary", "arbitrary"), vmem_limit_bytes=VMEM_LIMIT),
        name="proj",
    )(x, *consts, *tab_list)


Q_TILE = MXU_COLS
TILES_PER_BRANCH = 2


def _causal_lengths(n_tiles, total):
    n_groups = -(-n_tiles // TILES_PER_BRANCH)
    return [min(total, ((g + 1) * TILES_PER_BRANCH + 1) * Q_TILE) for g in range(n_groups)]


def _causal_valid(i, kv_len, pad_front):
    qpos = (i + 1) * Q_TILE + lax.broadcasted_iota(jnp.int32, (Q_TILE, kv_len), 0)
    kpos = lax.broadcasted_iota(jnp.int32, (Q_TILE, kv_len), 1)
    return (kpos <= qpos) & (kpos >= pad_front)


def _softmax_pv(s, v):
    p = jnp.exp(s - jnp.max(s, axis=1, keepdims=True))
    return _dot(p.astype(BF16), v) / jnp.sum(p, axis=1, keepdims=True)


def _mla_prompt_kernel(q_ref, k_ref, v_ref, o_ref, *, pad_front, lens):
    i = pl.program_id(1)
    for g, kv_len in enumerate(lens):

        @pl.when(i // TILES_PER_BRANCH == g)
        def _(kv_len=kv_len):
            bias = jnp.where(_causal_valid(i, kv_len, pad_front), 0.0, NEG_BIG)
            for h in range(MLA_HEADS):
                hs = slice(h * HEAD_PAD, (h + 1) * HEAD_PAD)
                s = _dot_nt(q_ref[0, :, hs], k_ref[0, :kv_len, hs]) + bias
                o_ref[0, :, hs] = _softmax_pv(s, v_ref[0, :kv_len, hs]).astype(BF16)


def _mla_prompt(qm, km, vm, pad_front, n_rows):
    B, T, W = qm.shape
    n_tiles = n_rows // Q_TILE
    return pl.pallas_call(
        functools.partial(_mla_prompt_kernel, pad_front=pad_front, lens=_causal_lengths(n_tiles, T)),
        grid=(B, n_tiles),
        in_specs=[pl.BlockSpec((1, Q_TILE, W), lambda b, i: (b, i + 1, 0)),
                  pl.BlockSpec((1, T, W), lambda b, i: (b, 0, 0)),
                  pl.BlockSpec((1, T, W), lambda b, i: (b, 0, 0))],
        out_specs=pl.BlockSpec((1, Q_TILE, W), lambda b, i: (b, i, 0)),
        out_shape=jax.ShapeDtypeStruct((B, n_rows, W), BF16),
        compiler_params=pltpu.CompilerParams(dimension_semantics=("arbitrary", "arbitrary"), vmem_limit_bytes=VMEM_LIMIT),
        name="mla_prompt",
    )(qm, km, vm)


KEY_NEG_INF = INT_MIN + 0x7FFFFF

def _key_to_float(key):
    return pltpu.bitcast(jnp.where(key >= 0, key, key ^ 0x7FFFFFFF), F32)


def _count(mask):
    return jnp.sum(jnp.where(mask, 1.0, 0.0), axis=1, keepdims=True)


def _kth_largest(s, k):
    kf = jnp.float32(k)
    t0 = jnp.where(_count(s >= 0.0) >= kf, 0, INT_MIN).astype(jnp.int32)

    def body(it, t):
        cand = t | (jnp.int32(1) << (30 - it))
        ok = (_count(s >= _key_to_float(cand)) >= kf) | (cand <= KEY_NEG_INF)
        return jnp.where(ok, cand, t)

    return _key_to_float(lax.fori_loop(0, 31, body, t0))


def _topk_mask(s, k, tri_ref):
    n = s.shape[1]
    t = _kth_largest(s, k)
    gt = s > t
    eq = s == t
    need = jnp.float32(k) - _count(gt)
    tri = tri_ref[...]
    run = jnp.zeros((s.shape[0], 1), F32)
    parts = []
    for c in range(n // LANES):
        sl = slice(c * LANES, (c + 1) * LANES)
        e = jnp.where(eq[:, sl], 1.0, 0.0)
        pc = _dot(e.astype(BF16), tri) + run
        parts.append(gt[:, sl] | (eq[:, sl] & (pc <= need)))
        run = run + jnp.sum(e, axis=1, keepdims=True)
    return jnp.concatenate(parts, axis=1)


def _dsa_prompt_kernel(qi_ref, wi_ref, ki_ref, qd_ref, kd_ref, vd_ref, tri_ref, o_ref, *, pad_front, topk, lens):
    i = pl.program_id(1)
    group = DSA_HEADS // DSA_KV_HEADS
    o_ref[...] = jnp.zeros(o_ref.shape, o_ref.dtype)
    for g, kv_len in enumerate(lens):

        @pl.when(i // TILES_PER_BRANCH == g)
        def _(kv_len=kv_len):
            valid = _causal_valid(i, kv_len, pad_front)
            ki = ki_ref[0, :kv_len, :]
            wi = wi_ref[0] * IDX_SCALE
            sc = jnp.zeros((Q_TILE, kv_len), F32)
            for h in range(IDX_HEADS):
                d = _dot_nt(qi_ref[0, :, h * IDX_DIM:(h + 1) * IDX_DIM], ki)
                sc = sc + jnp.maximum(d, 0.0) * wi[:, h:h + 1]
            sel = _topk_mask(jnp.where(valid, sc, -jnp.inf), topk, tri_ref) & valid
            bias = jnp.where(sel, 0.0, NEG_BIG)
            for n in range(DSA_KV_HEADS):
                ns = slice(n * DSA_HEAD_DIM, (n + 1) * DSA_HEAD_DIM)
                k = kd_ref[0, :kv_len, ns]
                v = vd_ref[0, :kv_len, ns]
                for j in range(group):
                    h = n * group + j
                    s = _dot_nt(qd_ref[0, :, h * DSA_HEAD_DIM:(h + 1) * DSA_HEAD_DIM], k) + bias
                    o_ref[0, :, h * HEAD_PAD:h * HEAD_PAD + DSA_HEAD_DIM] = _softmax_pv(s, v).astype(BF16)


def _dsa_prompt(qi, wi, ki, qd, kd, vd, tri, pad_front, topk, n_rows):
    B, T, _ = qi.shape
    n_tiles = n_rows // Q_TILE

    def tile(w):
        return pl.BlockSpec((1, Q_TILE, w), lambda b, i: (b, i + 1, 0))

    def full(w):
        return pl.BlockSpec((1, T, w), lambda b, i: (b, 0, 0))

    return pl.pallas_call(
        functools.partial(_dsa_prompt_kernel, pad_front=pad_front, topk=topk, lens=_causal_lengths(n_tiles, T)),
        grid=(B, n_tiles),
        in_specs=[tile(qi.shape[2]), tile(wi.shape[2]), full(ki.shape[2]), tile(qd.shape[2]), full(kd.shape[2]),
                  full(vd.shape[2]), _const_spec(tri)],
        out_specs=pl.BlockSpec((1, Q_TILE, DSA_HEADS * HEAD_PAD), lambda b, i: (b, i, 0)),
        out_shape=jax.ShapeDtypeStruct((B, n_rows, DSA_HEADS * HEAD_PAD), BF16),
        compiler_params=pltpu.CompilerParams(dimension_semantics=("arbitrary", "arbitrary"), vmem_limit_bytes=VMEM_LIMIT),
        name="dsa_prompt",
    )(qi, wi, ki, qd, kd, vd, tri)


def _merge_ffn_kernel(x_ref, om_ref, od_ref, gin_ref, wg_ref, wba_ref, wbb_ref, wout_ref, gffn_ref, wgate_ref, wup_ref,
                      wdown_ref, y_ref):
    x = x_ref[0]
    d = x.shape[1]
    hb = _rms(x, gin_ref[...]).astype(BF16)
    ga = jax.nn.sigmoid(_dot(hb, wg_ref[:, :d]))
    gb = jax.nn.sigmoid(_dot(hb, wg_ref[:, d:]))
    merged = ga * _dot(om_ref[0], wba_ref[...]) + gb * _dot(od_ref[0], wbb_ref[...])
    x1 = x + _dot(merged.astype(BF16), wout_ref[...])
    h2 = _rms(x1, gffn_ref[...]).astype(BF16)
    u = jax.nn.silu(_dot(h2, wgate_ref[...])) * _dot(h2, wup_ref[...])
    y_ref[0] = x1 + _dot(u.astype(BF16), wdown_ref[...])


def _merge_ffn(x, om, od, wts, first_row):
    B, T, D = x.shape
    n_rows = om.shape[1]
    tm = _row_tile(n_rows)
    assert first_row % tm == 0
    off = first_row // tm
    consts = [wts[k] for k in ("g_in", "wg", "w_branch_a", "w_branch_b", "w_out", "g_ffn", "w_gate", "w_up", "w_down")]

    def row_spec(w):
        return pl.BlockSpec((1, tm, w), lambda b, i: (b, i, 0))

    return pl.pallas_call(
        _merge_ffn_kernel,
        grid=(B, n_rows // tm),
        in_specs=[pl.BlockSpec((1, tm, D), lambda b, i: (b, i + off, 0)), row_spec(om.shape[2]), row_spec(od.shape[2])]
        + [_const_spec(a) for a in consts],
        out_specs=row_spec(D),
        out_shape=jax.ShapeDtypeStruct((B, n_rows, D), F32),
        compiler_params=pltpu.CompilerParams(dimension_semantics=("arbitrary", "arbitrary"), vmem_limit_bytes=VMEM_LIMIT),
        name="merge_ffn",
    )(x, om, od, *consts)


def _absorb_kernel(qm_ref, gk_ref, gsw_ref, wuk_ref, p1_ref, p2_ref, qabs_ref, qq_ref):
    q = qm_ref[...].astype(F32)
    a = q * gk_ref[...]
    b = q * gsw_ref[...]
    lane = lax.broadcasted_iota(jnp.int32, q.shape, 1) % HEAD_PAD
    n_hi, n_lo = _split_bf16(jnp.where(lane >= MLA_ROPE, a, 0.0))
    a_hi, a_lo = _split_bf16(a)
    b_hi, b_lo = _split_bf16(b)
    p1, p2 = p1_ref[...], p2_ref[...]
    for h in range(MLA_HEADS):
        hs = slice(h * HEAD_PAD, (h + 1) * HEAD_PAD)
        w = wuk_ref[:, hs]
        qabs_ref[h] = _dot_nt(n_hi[:, hs], w) + _dot_nt(n_lo[:, hs], w)
        qq_ref[h] = _dot(a_hi[:, hs], p1) + _dot(a_lo[:, hs], p1) + _dot(b_hi[:, hs], p2) + _dot(b_lo[:, hs], p2)


def _absorb(qm, wts):
    n = qm.shape[0]
    return pl.pallas_call(
        _absorb_kernel,
        out_shape=[jax.ShapeDtypeStruct((MLA_HEADS, n, MLA_KV_LORA), F32),
                   jax.ShapeDtypeStruct((MLA_HEADS, n, 2 * MLA_ROPE), F32)],
        name="absorb",
    )(qm, wts["g_mla_k"], wts["g_mla_k_swap"], wts["w_uk"], wts["p_same"], wts["p_cross"])


def _head_rows(full, width):
    row = lax.broadcasted_iota(jnp.int32, (full.shape[0], width), 0)
    out = jnp.zeros((full.shape[0], width), F32)
    for h in range(full.shape[0]):
        out = out + jnp.where(row == h, full[:, h * width:(h + 1) * width], 0.0)
    return out


def _page_copies(pt_ref, hbm_refs, bufs, sems, seq, chunk, slot, n_pages, page_of=None):
    copies = []
    for a, (hbm, buf) in enumerate(zip(hbm_refs, bufs)):
        for p in range(n_pages):
            page = pt_ref[seq, chunk * n_pages + p] if page_of is None else page_of
            copies.append(pltpu.make_async_copy(hbm.at[0, page], buf.at[slot, p], sems.at[slot, a]))
    return copies


def _stream_pages(pt_ref, hbm_refs, bufs, sems, n_pages):
    b, j = pl.program_id(0), pl.program_id(1)
    nb, nj = pl.num_programs(0), pl.num_programs(1)
    step = b * nj + j
    slot = step % 2

    @pl.when(step == 0)
    def _():
        for cp in _page_copies(pt_ref, hbm_refs, bufs, sems, b, j, slot, n_pages):
            cp.start()

    @pl.when(step + 1 < nb * nj)
    def _():
        wrap = j + 1 == nj
        for cp in _page_copies(pt_ref, hbm_refs, bufs, sems, jnp.where(wrap, b + 1, b), jnp.where(wrap, 0, j + 1),
                               1 - slot, n_pages):
            cp.start()

    for cp in _page_copies(pt_ref, hbm_refs, bufs, sems, b, j, slot, n_pages, page_of=0):
        cp.wait()
    return slot


def _mla_sample_kernel(pt_ref, lat_hbm, kpe_hbm, idx_hbm, qabs_ref, qq_ref, qm8_ref, km8_ref, vm8_ref, qi8_ref, wi8_ref,
                       wukt_ref, wuv_ref, cos_ref, sin_ref, o_ref, sc_ref, m_ref, l_ref, acc_ref, lat_buf, kpe_buf,
                       idx_buf, sems, *, n_pages):
    j = pl.program_id(1)
    slot = _stream_pages(pt_ref, (lat_hbm, kpe_hbm, idx_hbm), (lat_buf, kpe_buf, idx_buf), sems, n_pages)

    @pl.when(j == 0)
    def _():
        m_ref[...] = jnp.full(m_ref.shape, NEG_BIG, F32)
        l_ref[...] = jnp.zeros(l_ref.shape, F32)
        acc_ref[...] = jnp.zeros(acc_ref.shape, F32)

    n_nope = MLA_HEADS * MLA_NOPE
    lhs = jnp.concatenate([wukt_ref[...], qabs_ref[0].astype(BF16)], axis=0)
    qq = qq_ref[0].astype(BF16)
    wi = wi8_ref[0] * IDX_SCALE
    half = n_pages // 2
    hw = half * PAGE_SIZE
    parts = []
    for s in range(2):
        pages = range(s * half, (s + 1) * half)
        cols = slice(s * hw, (s + 1) * hw)
        cb = lat_buf[slot, s * half:(s + 1) * half].reshape(hw, MLA_KV_LORA).astype(BF16)
        pe_t = jnp.concatenate([kpe_buf[slot, p] for p in pages], axis=1)
        ki_t = jnp.concatenate([idx_buf[slot, p] for p in pages], axis=1).astype(BF16)
        r = _dot_nt(lhs, cb)
        kn = r[:n_nope]
        ss_nope = jnp.sum((kn * kn).reshape(MLA_NOPE, MLA_HEADS, hw), axis=0)
        ss_pe = jnp.sum(pe_t * pe_t, axis=0, keepdims=True)
        feat = jnp.concatenate([pe_t * cos_ref[j, :, cols], pe_t * sin_ref[j, :, cols]], axis=0).astype(BF16)
        logit = (r[n_nope:n_nope + MLA_HEADS] + _dot(qq, feat)) * lax.rsqrt((ss_nope + ss_pe) * (1.0 / MLA_QK) + EPS)
        m_s = jnp.max(logit, axis=1, keepdims=True)
        p = jnp.exp(logit - m_s)
        parts.append((m_s, jnp.sum(p, axis=1, keepdims=True), _dot(p.astype(BF16), cb)))
        d = _dot(qi8_ref[0], ki_t)
        sc_ref[0, :, cols] = jnp.sum(jnp.maximum(d, 0.0) * wi, axis=0, keepdims=True)

    m_old = m_ref[...]
    m_new = jnp.maximum(m_old, jnp.maximum(parts[0][0], parts[1][0]))
    a = jnp.exp(m_old - m_new)
    l_new = a * l_ref[...]
    acc_new = a * acc_ref[...]
    for m_s, l_s, acc_s in parts:
        e = jnp.exp(m_s - m_new)
        l_new = l_new + e * l_s
        acc_new = acc_new + e * acc_s
    m_ref[...] = m_new
    l_ref[...] = l_new
    acc_ref[...] = acc_new

    @pl.when(j == pl.num_programs(1) - 1)
    def _():
        s_new = jnp.sum(qm8_ref[0].astype(F32) * km8_ref[0].astype(F32), axis=1, keepdims=True)
        m_f = jnp.maximum(m_ref[...], s_new)
        a_f = jnp.exp(m_ref[...] - m_f)
        p_new = jnp.exp(s_new - m_f)
        l_f = a_f * l_ref[...] + p_new
        o_past = _head_rows(_dot((a_f * acc_ref[...]).astype(BF16), wuv_ref[...]), HEAD_PAD)
        o_ref[0] = ((o_past + p_new * vm8_ref[0].astype(F32)) / l_f).astype(BF16)


def _mla_sample(page_table, cache_lat, cache_kpe, cache_idx, qabs, qq, qm8, km8, vm8, qi8, wi8, wts, cos_t, sin_t, n_pages):
    nb, n_tot = page_table.shape
    n_chunks = n_tot // n_pages
    ch = n_pages * PAGE_SIZE

    def seq_spec(a):
        return pl.BlockSpec((1,) + a.shape[1:], lambda b, j, pt: (b, 0, 0))

    def const_spec(a):
        nd = a.ndim
        return pl.BlockSpec(a.shape, lambda b, j, pt: (0,) * nd, pipeline_mode=pl.Buffered(1))

    caches = [cache_lat, cache_kpe, cache_idx]
    seq_in = [qabs, qq, qm8, km8, vm8, qi8, wi8]
    consts = [wts["w_uk_t"], wts["w_uv"], cos_t, sin_t]
    in_specs = ([pl.BlockSpec(memory_space=pl.ANY)] * len(caches) + [seq_spec(a) for a in seq_in]
                + [const_spec(a) for a in consts])
    grid_spec = pltpu.PrefetchScalarGridSpec(
        num_scalar_prefetch=1, grid=(nb, n_chunks), in_specs=in_specs,
        out_specs=[pl.BlockSpec((1, MLA_HEADS, HEAD_PAD), lambda b, j, pt: (b, 0, 0)),
                   pl.BlockSpec((1, 1, ch), lambda b, j, pt: (b, 0, j))],
        scratch_shapes=[pltpu.VMEM((MLA_HEADS, 1), F32), pltpu.VMEM((MLA_HEADS, 1), F32),
                        pltpu.VMEM((MLA_HEADS, MLA_KV_LORA), F32)]
        + [pltpu.VMEM((2, n_pages) + a.shape[2:], a.dtype) for a in caches]
        + [pltpu.SemaphoreType.DMA((2, len(caches)))])
    return pl.pallas_call(
        functools.partial(_mla_sample_kernel, n_pages=n_pages),
        grid_spec=grid_spec,
        out_shape=[jax.ShapeDtypeStruct((nb, MLA_HEADS, HEAD_PAD), BF16),
                   jax.ShapeDtypeStruct((nb, 1, n_tot * PAGE_SIZE), F32)],
        compiler_params=pltpu.CompilerParams(dimension_semantics=("arbitrary", "arbitrary"), vmem_limit_bytes=VMEM_LIMIT),
        name="mla_sample",
    )(page_table, *caches, *seq_in, *consts)


def _select_sample_kernel(sc_ref, qi_ref, ki_ref, wi_ref, tri_ref, bias_ref, *, topk):
    n = sc_ref.shape[0]
    ki = ki_ref[...].astype(F32)
    wi = wi_ref[...] * IDX_SCALE
    s_new = jnp.zeros((n, 1), F32)
    for h in range(IDX_HEADS):
        d = jnp.sum(qi_ref[:, h * IDX_DIM:(h + 1) * IDX_DIM].astype(F32) * ki, axis=1, keepdims=True)
        s_new = s_new + jnp.maximum(d, 0.0) * wi[:, h:h + 1]
    new_lane = lax.broadcasted_iota(jnp.int32, (n, LANES), 1) == 0
    tail = jnp.where(new_lane, s_new, -jnp.inf)
    s = jnp.concatenate([sc_ref[...], tail], axis=1)
    valid = lax.broadcasted_iota(jnp.int32, s.shape, 1) <= sc_ref.shape[1]
    sel = _topk_mask(s, topk, tri_ref) & valid
    bias_ref[...] = jnp.where(sel, 0.0, NEG_BIG)


def _select_sample(scores, qi, ki, wi, tri, topk):
    n, p = scores.shape
    return pl.pallas_call(
        functools.partial(_select_sample_kernel, topk=topk),
        out_shape=jax.ShapeDtypeStruct((n, p + LANES), F32),
        compiler_params=pltpu.CompilerParams(vmem_limit_bytes=VMEM_LIMIT),
        name="select_sample",
    )(scores, qi, ki, wi, tri)


def _dsa_sample_kernel(pt_ref, *refs, n_pages):
    kp = refs[:n_pages]
    vp = refs[n_pages:2 * n_pages]
    bias_ref, bnew_ref, qd8_ref, kn8_ref, vn8_ref, o_ref, m_ref, l_ref, acc_ref = refs[2 * n_pages:]
    j = pl.program_id(1)
    group = DSA_HEADS // DSA_KV_HEADS

    @pl.when(j == 0)
    def _():
        m_ref[...] = jnp.full(m_ref.shape, NEG_BIG, F32)
        l_ref[...] = jnp.zeros(l_ref.shape, F32)
        acc_ref[...] = jnp.zeros(acc_ref.shape, F32)

    q = qd8_ref[0]
    ch = n_pages * PAGE_SIZE
    kv_head = lax.broadcasted_iota(jnp.int32, (DSA_HEADS, ch), 0) // group
    s = jnp.zeros((DSA_HEADS, ch), F32)
    for n in range(DSA_KV_HEADS):
        k_t = jnp.concatenate([r[n] for r in kp], axis=1).astype(BF16)
        s = jnp.where(kv_head == n, _dot(q, k_t), s)
    s = s + bias_ref[0]
    m_new = jnp.maximum(m_ref[...], jnp.max(s, axis=1, keepdims=True))
    a = jnp.exp(m_ref[...] - m_new)
    p = jnp.exp(s - m_new)
    l_ref[...] = a * l_ref[...] + jnp.sum(p, axis=1, keepdims=True)
    m_ref[...] = m_new
    pb = p.astype(BF16)
    for n in range(DSA_KV_HEADS):
        v_t = jnp.concatenate([r[n] for r in vp], axis=1).astype(BF16)
        acc_ref[n] = a * acc_ref[n] + _dot_nt(pb, v_t)

    @pl.when(j == pl.num_programs(1) - 1)
    def _():
        s_new = jnp.sum(q.astype(F32) * kn8_ref[0].astype(BF16).astype(F32), axis=1, keepdims=True) + bnew_ref[0]
        m_f = jnp.maximum(m_ref[...], s_new)
        a_f = jnp.exp(m_ref[...] - m_f)
        p_new = jnp.exp(s_new - m_f)
        l_f = a_f * l_ref[...] + p_new
        row_head = lax.broadcasted_iota(jnp.int32, (DSA_HEADS, DSA_HEAD_DIM), 0) // group
        o = jnp.zeros((DSA_HEADS, DSA_HEAD_DIM), F32)
        for n in range(DSA_KV_HEADS):
            o = o + jnp.where(row_head == n, acc_ref[n], 0.0)
        o_ref[0] = (a_f * o + p_new * vn8_ref[0]) / l_f


def _dsa_sample(page_table, cache_k, cache_v, bias, bias_new, qd8, kn8, vn8, n_pages):
    nb, n_tot = page_table.shape
    n_chunks = n_tot // n_pages
    ch = n_pages * PAGE_SIZE

    def page_spec(p):
        return pl.BlockSpec((None, None, DSA_KV_HEADS, DSA_HEAD_DIM, PAGE_SIZE),
                            lambda b, j, pt, p=p: (0, pt[b, j * n_pages + p], 0, 0, 0))

    def seq_spec(a):
        return pl.BlockSpec((1,) + a.shape[1:], lambda b, j, pt: (b, 0, 0))

    in_specs = ([page_spec(p) for p in range(n_pages)] * 2 + [pl.BlockSpec((1, 1, ch), lambda b, j, pt: (b, 0, j))]
                + [seq_spec(a) for a in (bias_new, qd8, kn8, vn8)])
    grid_spec = pltpu.PrefetchScalarGridSpec(
        num_scalar_prefetch=1, grid=(nb, n_chunks), in_specs=in_specs,
        out_specs=pl.BlockSpec((1, DSA_HEADS, DSA_HEAD_DIM), lambda b, j, pt: (b, 0, 0)),
        scratch_shapes=[pltpu.VMEM((DSA_HEADS, 1), F32), pltpu.VMEM((DSA_HEADS, 1), F32),
                        pltpu.VMEM((DSA_KV_HEADS, DSA_HEADS, DSA_HEAD_DIM), F32)])
    return pl.pallas_call(
        functools.partial(_dsa_sample_kernel, n_pages=n_pages),
        grid_spec=grid_spec,
        out_shape=jax.ShapeDtypeStruct((nb, DSA_HEADS, DSA_HEAD_DIM), F32),
        compiler_params=pltpu.CompilerParams(dimension_semantics=("arbitrary", "arbitrary"), vmem_limit_bytes=VMEM_LIMIT),
        name="dsa_sample",
    )(page_table, *([cache_k] * n_pages), *([cache_v] * n_pages), bias, bias_new, qd8, kn8, vn8)


def _pad_heads(w, lead, n_heads, width, offset=0):
    w = w.reshape(lead + (n_heads, width))
    w = jnp.pad(w, [(0, 0)] * len(lead) + [(0, 0), (offset, HEAD_PAD - width - offset)])
    return w.reshape(lead + (n_heads * HEAD_PAD,))


def _block_diag_ones(width):
    r = jnp.arange(MXU_COLS) // width
    return (r[:, None] == r[None, :]).astype(BF16)


def _prepare_weights(l, g_in, w_in, g_q_lat, w_uq, g_mla_q, g_kv_lat, w_uk, w_uv, g_mla_k, g_dsa_q, g_dsa_k, w_branch_a,
                     w_branch_b, w_out, g_ffn, w_gate, w_up, w_down):
    d = w_in.shape[1]
    splits = (MLA_Q_LORA, MLA_KV_LORA, MLA_ROPE, DSA_HEADS * DSA_HEAD_DIM, DSA_KV_HEADS * DSA_HEAD_DIM,
              DSA_KV_HEADS * DSA_HEAD_DIM, IDX_HEADS * IDX_DIM, IDX_DIM, IDX_HEADS, d, d)
    pts, acc = [], 0
    for s in splits[:-1]:
        acc += s
        pts.append(acc)
    (w_ql, w_kv, w_pe, w_qd, w_kd, w_vd, w_qi, w_ki, w_wi, w_ga, w_gb) = jnp.split(w_in[l], pts, axis=1)
    w_small = jnp.concatenate([w_pe, w_ki, w_wi, jnp.zeros((d, LANES - _S_WI - IDX_HEADS), F32)], axis=1)
    row = lambda g: g.reshape(1, -1).astype(F32)
    eye_pe = jnp.eye(MLA_ROPE, dtype=F32)
    half_pe = MLA_ROPE // 2
    return {
        "g_in": row(g_in[l]),
        "wa": jnp.concatenate([w_ql, w_kv, w_qd, w_kd, w_vd, w_qi, w_small], axis=1).astype(BF16),
        "wg": jnp.concatenate([w_ga, w_gb], axis=1).astype(BF16),
        "g_q_lat": row(g_q_lat[l]),
        "w_uq": _pad_heads(w_uq[l].reshape(MLA_Q_LORA, -1), (MLA_Q_LORA,), MLA_HEADS, MLA_QK).astype(BF16),
        "g_mla_q": row(_pad_heads(jnp.tile(g_mla_q[l], MLA_HEADS), (), MLA_HEADS, MLA_QK)),
        "g_kv_lat": row(g_kv_lat[l]),
        "w_uk": _pad_heads(w_uk[l].reshape(MLA_KV_LORA, -1), (MLA_KV_LORA,), MLA_HEADS, MLA_NOPE, MLA_ROPE).astype(BF16),
        "t_pe": _pad_heads(jnp.tile(eye_pe, (1, MLA_HEADS)), (MLA_ROPE,), MLA_HEADS, MLA_ROPE).astype(BF16),
        "w_uv": _pad_heads(w_uv[l].reshape(MLA_KV_LORA, -1), (MLA_KV_LORA,), MLA_HEADS, MLA_V).astype(BF16),
        "g_mla_k": row(_pad_heads(jnp.tile(g_mla_k[l], MLA_HEADS), (), MLA_HEADS, MLA_QK)),
        "g_dsa_q": row(jnp.tile(g_dsa_q[l], DSA_HEADS)),
        "g_dsa_k": row(jnp.tile(g_dsa_k[l], DSA_KV_HEADS)),
        "g_mla_k_swap": row(_pad_heads(jnp.tile(jnp.concatenate([g_mla_k[l][half_pe:MLA_ROPE], g_mla_k[l][:half_pe]]),
                                                MLA_HEADS), (), MLA_HEADS, MLA_ROPE)),
        "w_uk_t": jnp.transpose(w_uk[l], (2, 1, 0)).reshape(MLA_HEADS * MLA_NOPE, MLA_KV_LORA).astype(BF16),
        "eye_pe": eye_pe.astype(BF16),
        "p_same": jnp.pad(eye_pe, ((0, HEAD_PAD - MLA_ROPE), (0, MLA_ROPE))).astype(BF16),
        "p_cross": jnp.pad(jnp.concatenate([jnp.roll(eye_pe, half_pe, axis=0)[:, :half_pe],
                                            -jnp.roll(eye_pe, half_pe, axis=0)[:, half_pe:]], axis=1),
                           ((0, HEAD_PAD - MLA_ROPE), (MLA_ROPE, 0))).astype(BF16),
        "bd128": _block_diag_ones(HEAD_PAD),
        "bd64": _block_diag_ones(DSA_HEAD_DIM),
        "w_branch_a": _pad_heads(w_branch_a[l].T, (d,), MLA_HEADS, MLA_V).T.astype(BF16),
        "w_branch_b": _pad_heads(w_branch_b[l].T, (d,), DSA_HEADS, DSA_HEAD_DIM).T.astype(BF16),
        "w_out": w_out[l].astype(BF16),
        "g_ffn": row(g_ffn[l]),
        "w_gate": w_gate[l].astype(BF16),
        "w_up": w_up[l].astype(BF16),
        "w_down": w_down[l].astype(BF16),
    }


def _head_tables(pos, rot, width):
    half = rot // 2
    inv_freq = ROPE_THETA ** (-jnp.arange(half, dtype=F32) / half)
    ang = pos.astype(F32)[:, None] * inv_freq[None, :]
    cos, sin = jnp.cos(ang), jnp.sin(ang)
    t = pos.shape[0]
    c = jnp.concatenate([cos, cos, jnp.ones((t, width - rot), F32)], axis=1)
    s_lo = jnp.concatenate([-sin, jnp.zeros((t, width - half), F32)], axis=1)
    s_hi = jnp.concatenate([jnp.zeros((t, half), F32), sin, jnp.zeros((t, width - rot), F32)], axis=1)
    return c, s_lo, s_hi


def _rope_tables(pos):
    t = pos.shape[0]
    cm, slm, shm = (jnp.tile(a, (1, MLA_HEADS)) for a in _head_tables(pos, MLA_ROPE, HEAD_PAD))
    cd, sld, shd = (jnp.tile(a, (1, DSA_HEADS)) for a in _head_tables(pos, DSA_ROT, DSA_HEAD_DIM))
    ci, sli, shi = _head_tables(pos, IDX_ROT, IDX_DIM)
    pad = lambda a, fill: jnp.concatenate(
        [jnp.full((t, _S_KI), fill, F32), a, jnp.full((t, LANES - _S_WI), fill, F32)], axis=1)
    return {"cm": cm, "slm": slm, "shm": shm, "cd": cd, "sld": sld, "shd": shd,
            "cs": pad(ci, 1.0), "sls": pad(sli, 0.0), "shs": pad(shi, 0.0)}


def _upper_tri():
    r = jnp.arange(LANES)
    return (r[:, None] <= r[None, :]).astype(BF16)


def _prompt_layer(xp, wts, pad_front, topk, n_out_rows):
    T = xp.shape[1]
    tabs = _rope_tables(jnp.arange(T) - pad_front)
    qm, ckv, km, vm, qd, kd, kdb, vd, vdb, qi, sm = _project(xp, wts, tabs)
    om = _mla_prompt(qm, km, vm, pad_front, n_out_rows)
    kib = sm[:, :, _S_KI:_S_WI].astype(BF16)
    wi = sm[:, :, _S_WI:_S_WI + IDX_HEADS]
    od = _dsa_prompt(qi, wi, kib, qd, kdb, vdb, _upper_tri(), pad_front, topk, n_out_rows)
    y = _merge_ffn(xp, om, od, wts, T - n_out_rows)
    return y, ckv, sm, kd, vd


def _past_tables(past, n_pages):
    half = MLA_ROPE // 2
    inv_freq = ROPE_THETA ** (-jnp.arange(half, dtype=F32) / half)
    ang = jnp.arange(past).astype(F32)[:, None] * inv_freq[None, :]
    ch = n_pages * PAGE_SIZE

    def chunked(a):
        a = jnp.concatenate([a, a], axis=1).T.reshape(MLA_ROPE, past // ch, ch)
        return jnp.transpose(a, (1, 0, 2))

    return chunked(jnp.cos(ang)), chunked(jnp.sin(ang))


def _sample_layer(xs, wts, page_table, cache_lat, cache_kpe, cache_dk, cache_dv, cache_ik, topk):
    n, d = xs.shape
    n_tot = page_table.shape[1]
    past = n_tot * PAGE_SIZE
    n_pages = PAGES_PER_STEP if n_tot % PAGES_PER_STEP == 0 else n_tot
    assert n_pages % 2 == 0
    cache_kpe = jnp.swapaxes(cache_kpe, 2, 3)
    cache_ik = jnp.swapaxes(cache_ik, 2, 3)
    cache_dk = jnp.transpose(cache_dk, (0, 1, 3, 4, 2))
    cache_dv = jnp.transpose(cache_dv, (0, 1, 3, 4, 2))
    x3 = xs.reshape(1, n, d)
    qm, ckv, km, vm, qd, kd, _, vd, _, qi, sm = _project(x3, wts, _rope_tables(jnp.full((n,), past, jnp.int32)))
    qabs, qq = _absorb(qm[0], wts)
    per_head = lambda a, w: a[0].reshape(n, a.shape[2] // w, w)
    ki_new = sm[0, :, _S_KI:_S_WI].astype(BF16)
    wi_new = sm[0, :, _S_WI:_S_WI + IDX_HEADS]
    cos_t, sin_t = _past_tables(past, n_pages)
    qabs = jnp.pad(qabs, ((0, 2 * MLA_HEADS - qabs.shape[0]), (0, 0), (0, 0)))
    om8, scores = _mla_sample(page_table, cache_lat, cache_kpe, cache_ik, jnp.transpose(qabs, (1, 0, 2)),
                              jnp.transpose(qq, (1, 0, 2)), per_head(qm, HEAD_PAD), per_head(km, HEAD_PAD),
                              per_head(vm, HEAD_PAD), per_head(qi, IDX_DIM), wi_new.reshape(n, IDX_HEADS, 1), wts,
                              cos_t, sin_t, n_pages)
    bias = _select_sample(scores.reshape(n, past), qi[0], ki_new, wi_new, _upper_tri(), topk)
    group = DSA_HEADS // DSA_KV_HEADS
    od8 = _dsa_sample(page_table, cache_dk, cache_dv, bias[:, :past].reshape(n, 1, past),
                      bias[:, past:past + 1].reshape(n, 1, 1), per_head(qd, DSA_HEAD_DIM),
                      jnp.repeat(per_head(kd, DSA_HEAD_DIM), group, axis=1),
                      jnp.repeat(per_head(vd, DSA_HEAD_DIM), group, axis=1), n_pages)
    od = jnp.pad(od8, ((0, 0), (0, 0), (0, HEAD_PAD - DSA_HEAD_DIM))).astype(BF16)
    y = _merge_ffn(x3, om8.reshape(1, n, MLA_HEADS * HEAD_PAD), od.reshape(1, n, DSA_HEADS * HEAD_PAD), wts, 0)
    return y[0], ckv[0], sm[0], kd[0], vd[0]


def kernel(x_prompt, x_sample, cache_mla_latent, cache_mla_kpe, cache_dsa_k, cache_dsa_v, cache_idx_k, page_table,
           meta_tokens, g_in, w_in, g_q_lat, w_uq, g_mla_q, g_kv_lat, w_uk, w_uv, g_mla_k, g_dsa_q, g_dsa_k, w_branch_a,
           w_branch_b, w_out, g_ffn, w_gate, w_up, w_down):
    B, S, D = x_prompt.shape
    L = S + N_META
    pad_front = (-N_META) % Q_TILE
    topk_prompt = min(TOPK_MAX, L // 4)
    wts = _prepare_weights(0, g_in, w_in, g_q_lat, w_uq, g_mla_q, g_kv_lat, w_uk, w_uv, g_mla_k, g_dsa_q, g_dsa_k,
                           w_branch_a, w_branch_b, w_out, g_ffn, w_gate, w_up, w_down)
    meta = jnp.broadcast_to(meta_tokens.astype(F32)[None], (B, N_META, D))
    xp = jnp.concatenate([jnp.zeros((B, pad_front, D), F32), meta, x_prompt], axis=1)
    y_prompt, ckv, sm, kd, vd = _prompt_layer(xp, wts, pad_front, topk_prompt, S)
    real = slice(pad_front, pad_front + L)

    N, T, _ = x_sample.shape
    assert T == 1 and g_in.shape[0] == 1, "one trunk layer and one new token per running sequence"
    past = page_table.shape[1] * PAGE_SIZE
    topk_sample = min(TOPK_MAX, (past + T) // 4)
    y_s, ckv_s, sm_s, kd_s, vd_s = _sample_layer(x_sample[:, 0], wts, page_table, cache_mla_latent, cache_mla_kpe,
                                                 cache_dsa_k, cache_dsa_v, cache_idx_k, topk_sample)
    heads = (DSA_KV_HEADS, DSA_HEAD_DIM)
    return (y_prompt, y_s[:, None],
            ckv[None, :, real], ckv_s[None, :, None],
            sm[None, :, real, _S_KPE:_S_KI], sm_s[None, :, None, _S_KPE:_S_KI],
            kd[None, :, real].reshape((1, B, L) + heads), kd_s.reshape((1, N, 1) + heads),
            vd[None, :, real].reshape((1, B, L) + heads), vd_s.reshape((1, N, 1) + heads),
            sm[None, :, real, _S_KI:_S_WI], sm_s[None, :, None, _S_KI:_S_WI])
```

```python
import functools

import jax
import jax.numpy as jnp
from jax import lax
from jax.experimental import pallas as pl
from jax.experimental.pallas import tpu as pltpu

F32 = jnp.float32
BF16 = jnp.bfloat16

N_META = 16
ROPE_THETA = 500000.0
EPS = 1e-6
PAGE_SIZE = 128
MLA_HEADS = 8
MLA_Q_LORA = 384
MLA_KV_LORA = 256
MLA_NOPE = 64
MLA_ROPE = 32
MLA_QK = MLA_NOPE + MLA_ROPE
MLA_V = 64
MLA_SCALE = MLA_QK ** -0.5
DSA_HEADS = 8
DSA_KV_HEADS = 4
DSA_HEAD_DIM = 64
DSA_ROT = DSA_HEAD_DIM // 4
DSA_SCALE = DSA_HEAD_DIM ** -0.5
IDX_HEADS = 8
IDX_DIM = 64
IDX_ROT = IDX_DIM // 4
IDX_SCALE = (IDX_DIM * IDX_HEADS) ** -0.5
TOPK_MAX = 256

LANES = 128
MXU_COLS = 256
HEAD_PAD = LANES
NEG_BIG = -1e30
INT_MIN = -(2 ** 31)
VMEM_LIMIT = 56 * 1024 * 1024
PAGES_PER_STEP = 32

_C_QL = 0
_C_KV = _C_QL + MLA_Q_LORA
_C_QD = _C_KV + MLA_KV_LORA
_C_KD = _C_QD + DSA_HEADS * DSA_HEAD_DIM
_C_VD = _C_KD + DSA_KV_HEADS * DSA_HEAD_DIM
_C_QI = _C_VD + DSA_KV_HEADS * DSA_HEAD_DIM
_C_SM = _C_QI + IDX_HEADS * IDX_DIM
_C_END = _C_SM + LANES
_S_KPE = 0
_S_KI = _S_KPE + MLA_ROPE
_S_WI = _S_KI + IDX_DIM


def _dot(a, b):
    return jnp.dot(a, b, preferred_element_type=F32)


def _dot_nt(a, b):
    return lax.dot_general(a, b, (((1,), (1,)), ((), ())), preferred_element_type=F32)


def _split_bf16(x):
    hi = x.astype(BF16)
    lo = (x - hi.astype(F32)).astype(BF16)
    return hi, lo


def _rms(x, g):
    return x * lax.rsqrt(jnp.mean(x * x, axis=-1, keepdims=True) + EPS) * g


def _head_rms(x, bd, inv_n):
    hi, lo = _split_bf16(x * x)
    parts = []
    for c in range(x.shape[1] // MXU_COLS):
        sl = slice(c * MXU_COLS, (c + 1) * MXU_COLS)
        parts.append(_dot(hi[:, sl], bd) + _dot(lo[:, sl], bd))
    ss = parts[0] if len(parts) == 1 else jnp.concatenate(parts, axis=1)
    return x * lax.rsqrt(ss * inv_n + EPS)


def _rope(x, c, s_lo, s_hi, half):
    w = x.shape[-1]
    return x * c + pltpu.roll(x, w - half, 1) * s_lo + pltpu.roll(x, half, 1) * s_hi


def _proj_kernel(x_ref, gin_ref, wa_ref, gql_ref, wuq_ref, gmq_ref, gkv_ref, wuk_ref, tpe_ref, wuv_ref, gmk_ref,
                 gdq_ref, gdk_ref, bd128_ref, bd64_ref,
                 cm_ref, slm_ref, shm_ref, cd_ref, sld_ref, shd_ref, cs_ref, sls_ref, shs_ref,
                 qm_ref, ckv_ref, km_ref, vm_ref, qd_ref, kd_ref, kdb_ref, vd_ref, vdb_ref, qi_ref, sm_ref):
    hb = _rms(x_ref[0], gin_ref[...]).astype(BF16)

    def proj(a, b):
        return _dot(hb, wa_ref[:, a:b])

    bd128 = bd128_ref[...]
    bd64 = bd64_ref[...]
    cm, slm, shm = cm_ref[...], slm_ref[...], shm_ref[...]
    n_kd = DSA_KV_HEADS * DSA_HEAD_DIM

    ql = _rms(proj(_C_QL, _C_KV), gql_ref[...]).astype(BF16)
    q = _head_rms(_dot(ql, wuq_ref[...]), bd128, 1.0 / MLA_QK) * gmq_ref[...]
    qm_ref[0] = (_rope(q, cm, slm, shm, MLA_ROPE // 2) * MLA_SCALE).astype(BF16)

    c = _rms(proj(_C_KV, _C_QD), gkv_ref[...])
    ckv_ref[0] = c
    small = proj(_C_SM, _C_END)
    cb = c.astype(BF16)
    pe_hi, pe_lo = _split_bf16(small[:, _S_KPE:_S_KPE + MLA_ROPE])
    k = _dot(cb, wuk_ref[...]) + _dot(pe_hi, tpe_ref[...]) + _dot(pe_lo, tpe_ref[...])
    k = _head_rms(k, bd128, 1.0 / MLA_QK) * gmk_ref[...]
    km_ref[0] = _rope(k, cm, slm, shm, MLA_ROPE // 2).astype(BF16)
    vm_ref[0] = _dot(cb, wuv_ref[...]).astype(BF16)

    qd = _head_rms(proj(_C_QD, _C_KD), bd64, 1.0 / DSA_HEAD_DIM) * gdq_ref[...]
    qd_ref[0] = (_rope(qd, cd_ref[...], sld_ref[...], shd_ref[...], DSA_ROT // 2) * DSA_SCALE).astype(BF16)
    kd = _head_rms(proj(_C_KD, _C_VD), bd64, 1.0 / DSA_HEAD_DIM) * gdk_ref[...]
    kd = _rope(kd, cd_ref[:, :n_kd], sld_ref[:, :n_kd], shd_ref[:, :n_kd], DSA_ROT // 2)
    kd_ref[0] = kd
    kdb_ref[0] = kd.astype(BF16)
    vd = proj(_C_VD, _C_QI)
    vd_ref[0] = vd
    vdb_ref[0] = vd.astype(BF16)

    qi_ref[0] = _rope(proj(_C_QI, _C_SM), cd_ref[...], sld_ref[...], shd_ref[...], IDX_ROT // 2).astype(BF16)
    sm_ref[0] = _rope(small, cs_ref[...], sls_ref[...], shs_ref[...], IDX_ROT // 2)


def _row_tile(n):
    for t in (256, 128):
        if n % t == 0:
            return t
    raise ValueError(f"unsupported row count {n}")


def _const_spec(a):
    nd = a.ndim
    return pl.BlockSpec(a.shape, lambda *_: (0,) * nd, pipeline_mode=pl.Buffered(1))


def _project(x, wts, tabs):
    B, T, D = x.shape
    tm = _row_tile(T)
    consts = [wts[k] for k in ("g_in", "wa", "g_q_lat", "w_uq", "g_mla_q", "g_kv_lat", "w_uk", "t_pe", "w_uv", "g_mla_k",
                               "g_dsa_q", "g_dsa_k", "bd128", "bd64")]
    tab_list = [tabs[k] for k in ("cm", "slm", "shm", "cd", "sld", "shd", "cs", "sls", "shs")]

    def row_spec(w):
        return pl.BlockSpec((1, tm, w), lambda i, b: (b, i, 0))

    out_w = [(MLA_HEADS * HEAD_PAD, BF16), (MLA_KV_LORA, F32), (MLA_HEADS * HEAD_PAD, BF16), (MLA_HEADS * HEAD_PAD, BF16),
             (DSA_HEADS * DSA_HEAD_DIM, BF16), (DSA_KV_HEADS * DSA_HEAD_DIM, F32), (DSA_KV_HEADS * DSA_HEAD_DIM, BF16),
             (DSA_KV_HEADS * DSA_HEAD_DIM, F32), (DSA_KV_HEADS * DSA_HEAD_DIM, BF16), (IDX_HEADS * IDX_DIM, BF16),
             (LANES, F32)]
    return pl.pallas_call(
        _proj_kernel,
        grid=(T // tm, B),
        in_specs=[row_spec(D)] + [_const_spec(a) for a in consts]
        + [pl.BlockSpec((tm, t.shape[1]), lambda i, b: (i, 0)) for t in tab_list],
        out_specs=[row_spec(w) for w, _ in out_w],
        out_shape=[jax.ShapeDtypeStruct((B, T, w), dt) for w, dt in out_w],
        compiler_params=pltpu.CompilerParams(dimension_semantics=("arbitrary", "arbitrary"), vmem_limit_bytes=VMEM_LIMIT),
        name="proj",
    )(x, *consts, *tab_list)


Q_TILE = MXU_COLS
TILES_PER_BRANCH = 2


def _causal_lengths(n_tiles, total):
    n_groups = -(-n_tiles // TILES_PER_BRANCH)
    return [min(total, ((g + 1) * TILES_PER_BRANCH + 1) * Q_TILE) for g in range(n_groups)]


def _causal_valid(i, kv_len, pad_front):
    qpos = (i + 1) * Q_TILE + lax.broadcasted_iota(jnp.int32, (Q_TILE, kv_len), 0)
    kpos = lax.broadcasted_iota(jnp.int32, (Q_TILE, kv_len), 1)
    return (kpos <= qpos) & (kpos >= pad_front)


def _softmax_pv(s, v):
    p = jnp.exp(s - jnp.max(s, axis=1, keepdims=True))
    return _dot(p.astype(BF16), v) / jnp.sum(p, axis=1, keepdims=True)


def _mla_prompt_kernel(q_ref, k_ref, v_ref, o_ref, *, pad_front, lens):
    i = pl.program_id(1)
    for g, kv_len in enumerate(lens):

        @pl.when(i // TILES_PER_BRANCH == g)
        def _(kv_len=kv_len):
            bias = jnp.where(_causal_valid(i, kv_len, pad_front), 0.0, NEG_BIG)
            for h in range(MLA_HEADS):
                hs = slice(h * HEAD_PAD, (h + 1) * HEAD_PAD)
                s = _dot_nt(q_ref[0, :, hs], k_ref[0, :kv_len, hs]) + bias
                o_ref[0, :, hs] = _softmax_pv(s, v_ref[0, :kv_len, hs]).astype(BF16)


def _mla_prompt(qm, km, vm, pad_front, n_rows):
    B, T, W = qm.shape
    n_tiles = n_rows // Q_TILE
    return pl.pallas_call(
        functools.partial(_mla_prompt_kernel, pad_front=pad_front, lens=_causal_lengths(n_tiles, T)),
        grid=(B, n_tiles),
        in_specs=[pl.BlockSpec((1, Q_TILE, W), lambda b, i: (b, i + 1, 0)),
                  pl.BlockSpec((1, T, W), lambda b, i: (b, 0, 0)),
                  pl.BlockSpec((1, T, W), lambda b, i: (b, 0, 0))],
        out_specs=pl.BlockSpec((1, Q_TILE, W), lambda b, i: (b, i, 0)),
        out_shape=jax.ShapeDtypeStruct((B, n_rows, W), BF16),
        compiler_params=pltpu.CompilerParams(dimension_semantics=("arbitrary", "arbitrary"), vmem_limit_bytes=VMEM_LIMIT),
        name="mla_prompt",
    )(qm, km, vm)


KEY_NEG_INF = INT_MIN + 0x7FFFFF

def _key_to_float(key):
    return pltpu.bitcast(jnp.where(key >= 0, key, key ^ 0x7FFFFFFF), F32)


def _count(mask):
    return jnp.sum(jnp.where(mask, 1.0, 0.0), axis=1, keepdims=True)


def _kth_largest(s, k):
    kf = jnp.float32(k)
    t0 = jnp.where(_count(s >= 0.0) >= kf, 0, INT_MIN).astype(jnp.int32)

    def body(it, t):
        cand = t | (jnp.int32(1) << (30 - it))
        ok = (_count(s >= _key_to_float(cand)) >= kf) | (cand <= KEY_NEG_INF)
        return jnp.where(ok, cand, t)

    return _key_to_float(lax.fori_loop(0, 31, body, t0))


def _topk_mask(s, k, tri_ref):
    n = s.shape[1]
    t = _kth_largest(s, k)
    gt = s > t
    eq = s == t
    need = jnp.float32(k) - _count(gt)
    tri = tri_ref[...]
    run = jnp.zeros((s.shape[0], 1), F32)
    parts = []
    for c in range(n // LANES):
        sl = slice(c * LANES, (c + 1) * LANES)
        e = jnp.where(eq[:, sl], 1.0, 0.0)
        pc = _dot(e.astype(BF16), tri) + run
        parts.append(gt[:, sl] | (eq[:, sl] & (pc <= need)))
        run = run + jnp.sum(e, axis=1, keepdims=True)
    return jnp.concatenate(parts, axis=1)


def _dsa_prompt_kernel(qi_ref, wi_ref, ki_ref, qd_ref, kd_ref, vd_ref, tri_ref, o_ref, *, pad_front, topk, lens):
    i = pl.program_id(1)
    group = DSA_HEADS // DSA_KV_HEADS
    o_ref[...] = jnp.zeros(o_ref.shape, o_ref.dtype)
    for g, kv_len in enumerate(lens):

        @pl.when(i // TILES_PER_BRANCH == g)
        def _(kv_len=kv_len):
            valid = _causal_valid(i, kv_len, pad_front)
            ki = ki_ref[0, :kv_len, :]
            wi = wi_ref[0] * IDX_SCALE
            sc = jnp.zeros((Q_TILE, kv_len), F32)
            for h in range(IDX_HEADS):
                d = _dot_nt(qi_ref[0, :, h * IDX_DIM:(h + 1) * IDX_DIM], ki)
                sc = sc + jnp.maximum(d, 0.0) * wi[:, h:h + 1]
            sel = _topk_mask(jnp.where(valid, sc, -jnp.inf), topk, tri_ref) & valid
            bias = jnp.where(sel, 0.0, NEG_BIG)
            for n in range(DSA_KV_HEADS):
                ns = slice(n * DSA_HEAD_DIM, (n + 1) * DSA_HEAD_DIM)
                k = kd_ref[0, :kv_len, ns]
                v = vd_ref[0, :kv_len, ns]
                for j in range(group):
                    h = n * group + j
                    s = _dot_nt(qd_ref[0, :, h * DSA_HEAD_DIM:(h + 1) * DSA_HEAD_DIM], k) + bias
                    o_ref[0, :, h * HEAD_PAD:h * HEAD_PAD + DSA_HEAD_DIM] = _softmax_pv(s, v).astype(BF16)


def _dsa_prompt(qi, wi, ki, qd, kd, vd, tri, pad_front, topk, n_rows):
    B, T, _ = qi.shape
    n_tiles = n_rows // Q_TILE

    def tile(w):
        return pl.BlockSpec((1, Q_TILE, w), lambda b, i: (b, i + 1, 0))

    def full(w):
        return pl.BlockSpec((1, T, w), lambda b, i: (b, 0, 0))

    return pl.pallas_call(
        functools.partial(_dsa_prompt_kernel, pad_front=pad_front, topk=topk, lens=_causal_lengths(n_tiles, T)),
        grid=(B, n_tiles),
        in_specs=[tile(qi.shape[2]), tile(wi.shape[2]), full(ki.shape[2]), tile(qd.shape[2]), full(kd.shape[2]),
                  full(vd.shape[2]), _const_spec(tri)],
        out_specs=pl.BlockSpec((1, Q_TILE, DSA_HEADS * HEAD_PAD), lambda b, i: (b, i, 0)),
        out_shape=jax.ShapeDtypeStruct((B, n_rows, DSA_HEADS * HEAD_PAD), BF16),
        compiler_params=pltpu.CompilerParams(dimension_semantics=("arbitrary", "arbitrary"), vmem_limit_bytes=VMEM_LIMIT),
        name="dsa_prompt",
    )(qi, wi, ki, qd, kd, vd, tri)


def _merge_ffn_kernel(x_ref, om_ref, od_ref, gin_ref, wg_ref, wba_ref, wbb_ref, wout_ref, gffn_ref, wgate_ref, wup_ref,
                      wdown_ref, y_ref):
    x = x_ref[0]
    d = x.shape[1]
    hb = _rms(x, gin_ref[...]).astype(BF16)
    ga = jax.nn.sigmoid(_dot(hb, wg_ref[:, :d]))
    gb = jax.nn.sigmoid(_dot(hb, wg_ref[:, d:]))
    merged = ga * _dot(om_ref[0], wba_ref[...]) + gb * _dot(od_ref[0], wbb_ref[...])
    x1 = x + _dot(merged.astype(BF16), wout_ref[...])
    h2 = _rms(x1, gffn_ref[...]).astype(BF16)
    u = jax.nn.silu(_dot(h2, wgate_ref[...])) * _dot(h2, wup_ref[...])
    y_ref[0] = x1 + _dot(u.astype(BF16), wdown_ref[...])


def _merge_ffn(x, om, od, wts, first_row):
    B, T, D = x.shape
    n_rows = om.shape[1]
    tm = _row_tile(n_rows)
    assert first_row % tm == 0
    off = first_row // tm
    consts = [wts[k] for k in ("g_in", "wg", "w_branch_a", "w_branch_b", "w_out", "g_ffn", "w_gate", "w_up", "w_down")]

    def row_spec(w):
        return pl.BlockSpec((1, tm, w), lambda b, i: (b, i, 0))

    return pl.pallas_call(
        _merge_ffn_kernel,
        grid=(B, n_rows // tm),
        in_specs=[pl.BlockSpec((1, tm, D), lambda b, i: (b, i + off, 0)), row_spec(om.shape[2]), row_spec(od.shape[2])]
        + [_const_spec(a) for a in consts],
        out_specs=row_spec(D),
        out_shape=jax.ShapeDtypeStruct((B, n_rows, D), F32),
        compiler_params=pltpu.CompilerParams(dimension_semantics=("arbitrary", "arbitrary"), vmem_limit_bytes=VMEM_LIMIT),
        name="merge_ffn",
    )(x, om, od, *consts)


def _absorb_kernel(qm_ref, gk_ref, gsw_ref, wuk_ref, p1_ref, p2_ref, qabs_ref, qq_ref):
    q = qm_ref[...].astype(F32)
    a = q * gk_ref[...]
    b = q * gsw_ref[...]
    lane = lax.broadcasted_iota(jnp.int32, q.shape, 1) % HEAD_PAD
    n_hi, n_lo = _split_bf16(jnp.where(lane >= MLA_ROPE, a, 0.0))
    a_hi, a_lo = _split_bf16(a)
    b_hi, b_lo = _split_bf16(b)
    p1, p2 = p1_ref[...], p2_ref[...]
    for h in range(MLA_HEADS):
        hs = slice(h * HEAD_PAD, (h + 1) * HEAD_PAD)
        w = wuk_ref[:, hs]
        qabs_ref[h] = _dot_nt(n_hi[:, hs], w) + _dot_nt(n_lo[:, hs], w)
        qq_ref[h] = _dot(a_hi[:, hs], p1) + _dot(a_lo[:, hs], p1) + _dot(b_hi[:, hs], p2) + _dot(b_lo[:, hs], p2)


def _absorb(qm, wts):
    n = qm.shape[0]
    return pl.pallas_call(
        _absorb_kernel,
        out_shape=[jax.ShapeDtypeStruct((MLA_HEADS, n, MLA_KV_LORA), F32),
                   jax.ShapeDtypeStruct((MLA_HEADS, n, 2 * MLA_ROPE), F32)],
        name="absorb",
    )(qm, wts["g_mla_k"], wts["g_mla_k_swap"], wts["w_uk"], wts["p_same"], wts["p_cross"])


def _head_rows(full, width):
    row = lax.broadcasted_iota(jnp.int32, (full.shape[0], width), 0)
    out = jnp.zeros((full.shape[0], width), F32)
    for h in range(full.shape[0]):
        out = out + jnp.where(row == h, full[:, h * width:(h + 1) * width], 0.0)
    return out


def _page_copies(pt_ref, hbm_refs, bufs, sems, seq, chunk, slot, n_pages, page_of=None):
    copies = []
    for a, (hbm, buf) in enumerate(zip(hbm_refs, bufs)):
        for p in range(n_pages):
            page = pt_ref[seq, chunk * n_pages + p] if page_of is None else page_of
            copies.append(pltpu.make_async_copy(hbm.at[0, page], buf.at[slot, p], sems.at[slot, a]))
    return copies


def _stream_pages(pt_ref, hbm_refs, bufs, sems, n_pages):
    b, j = pl.program_id(0), pl.program_id(1)
    nb, nj = pl.num_programs(0), pl.num_programs(1)
    step = b * nj + j
    slot = step % 2

    @pl.when(step == 0)
    def _():
        for cp in _page_copies(pt_ref, hbm_refs, bufs, sems, b, j, slot, n_pages):
            cp.start()

    @pl.when(step + 1 < nb * nj)
    def _():
        wrap = j + 1 == nj
        for cp in _page_copies(pt_ref, hbm_refs, bufs, sems, jnp.where(wrap, b + 1, b), jnp.where(wrap, 0, j + 1),
                               1 - slot, n_pages):
            cp.start()

    for cp in _page_copies(pt_ref, hbm_refs, bufs, sems, b, j, slot, n_pages, page_of=0):
        cp.wait()
    return slot


def _mla_sample_kernel(pt_ref, lat_hbm, kpe_hbm, idx_hbm, qabs_ref, qq_ref, qm8_ref, km8_ref, vm8_ref, qi8_ref, wi8_ref,
                       wukt_ref, wuv_ref, cos_ref, sin_ref, o_ref, sc_ref, m_ref, l_ref, acc_ref, lat_buf, kpe_buf,
                       idx_buf, sems, *, n_pages):
    j = pl.program_id(1)
    slot = _stream_pages(pt_ref, (lat_hbm, kpe_hbm, idx_hbm), (lat_buf, kpe_buf, idx_buf), sems, n_pages)

    @pl.when(j == 0)
    def _():
        m_ref[...] = jnp.full(m_ref.shape, NEG_BIG, F32)
        l_ref[...] = jnp.zeros(l_ref.shape, F32)
        acc_ref[...] = jnp.zeros(acc_ref.shape, F32)

    n_nope = MLA_HEADS * MLA_NOPE
    lhs = jnp.concatenate([wukt_ref[...], qabs_ref[0].astype(BF16)], axis=0)
    qq = qq_ref[0].astype(BF16)
    wi = wi8_ref[0] * IDX_SCALE
    half = n_pages // 2
    hw = half * PAGE_SIZE
    parts = []
    for s in range(2):
        pages = range(s * half, (s + 1) * half)
        cols = slice(s * hw, (s + 1) * hw)
        cb = lat_buf[slot, s * half:(s + 1) * half].reshape(hw, MLA_KV_LORA).astype(BF16)
        pe_t = jnp.concatenate([kpe_buf[slot, p] for p in pages], axis=1)
        ki_t = jnp.concatenate([idx_buf[slot, p] for p in pages], axis=1).astype(BF16)
        r = _dot_nt(lhs, cb)
        kn = r[:n_nope]
        ss_nope = jnp.sum((kn * kn).reshape(MLA_NOPE, MLA_HEADS, hw), axis=0)
        ss_pe = jnp.sum(pe_t * pe_t, axis=0, keepdims=True)
        feat = jnp.concatenate([pe_t * cos_ref[j, :, cols], pe_t * sin_ref[j, :, cols]], axis=0).astype(BF16)
        logit = (r[n_nope:n_nope + MLA_HEADS] + _dot(qq, feat)) * lax.rsqrt((ss_nope + ss_pe) * (1.0 / MLA_QK) + EPS)
        m_s = jnp.max(logit, axis=1, keepdims=True)
        p = jnp.exp(logit - m_s)
        parts.append((m_s, jnp.sum(p, axis=1, keepdims=True), _dot(p.astype(BF16), cb)))
        d = _dot(qi8_ref[0], ki_t)
        sc_ref[0, :, cols] = jnp.sum(jnp.maximum(d, 0.0) * wi, axis=0, keepdims=True)

    m_old = m_ref[...]
    m_new = jnp.maximum(m_old, jnp.maximum(parts[0][0], parts[1][0]))
    a = jnp.exp(m_old - m_new)
    l_new = a * l_ref[...]
    acc_new = a * acc_ref[...]
    for m_s, l_s, acc_s in parts:
        e = jnp.exp(m_s - m_new)
        l_new = l_new + e * l_s
        acc_new = acc_new + e * acc_s
    m_ref[...] = m_new
    l_ref[...] = l_new
    acc_ref[...] = acc_new

    @pl.when(j == pl.num_programs(1) - 1)
    def _():
        s_new = jnp.sum(qm8_ref[0].astype(F32) * km8_ref[0].astype(F32), axis=1, keepdims=True)
        m_f = jnp.maximum(m_ref[...], s_new)
        a_f = jnp.exp(m_ref[...] - m_f)
        p_new = jnp.exp(s_new - m_f)
        l_f = a_f * l_ref[...] + p_new
        o_past = _head_rows(_dot((a_f * acc_ref[...]).astype(BF16), wuv_ref[...]), HEAD_PAD)
        o_ref[0] = ((o_past + p_new * vm8_ref[0].astype(F32)) / l_f).astype(BF16)


def _mla_sample(page_table, cache_lat, cache_kpe, cache_idx, qabs, qq, qm8, km8, vm8, qi8, wi8, wts, cos_t, sin_t, n_pages):
    nb, n_tot = page_table.shape
    n_chunks = n_tot // n_pages
    ch = n_pages * PAGE_SIZE

    def seq_spec(a):
        return pl.BlockSpec((1,) + a.shape[1:], lambda b, j, pt: (b, 0, 0))

    def const_spec(a):
        nd = a.ndim
        return pl.BlockSpec(a.shape, lambda b, j, pt: (0,) * nd, pipeline_mode=pl.Buffered(1))

    caches = [cache_lat, cache_kpe, cache_idx]
    seq_in = [qabs, qq, qm8, km8, vm8, qi8, wi8]
    consts = [wts["w_uk_t"], wts["w_uv"], cos_t, sin_t]
    in_specs = ([pl.BlockSpec(memory_space=pl.ANY)] * len(caches) + [seq_spec(a) for a in seq_in]
                + [const_spec(a) for a in consts])
    grid_spec = pltpu.PrefetchScalarGridSpec(
        num_scalar_prefetch=1, grid=(nb, n_chunks), in_specs=in_specs,
        out_specs=[pl.BlockSpec((1, MLA_HEADS, HEAD_PAD), lambda b, j, pt: (b, 0, 0)),
                   pl.BlockSpec((1, 1, ch), lambda b, j, pt: (b, 0, j))],
        scratch_shapes=[pltpu.VMEM((MLA_HEADS, 1), F32), pltpu.VMEM((MLA_HEADS, 1), F32),
                        pltpu.VMEM((MLA_HEADS, MLA_KV_LORA), F32)]
        + [pltpu.VMEM((2, n_pages) + a.shape[2:], a.dtype) for a in caches]
        + [pltpu.SemaphoreType.DMA((2, len(caches)))])
    return pl.pallas_call(
        functools.partial(_mla_sample_kernel, n_pages=n_pages),
        grid_spec=grid_spec,
        out_shape=[jax.ShapeDtypeStruct((nb, MLA_HEADS, HEAD_PAD), BF16),
                   jax.ShapeDtypeStruct((nb, 1, n_tot * PAGE_SIZE), F32)],
        compiler_params=pltpu.CompilerParams(dimension_semantics=("arbitrary", "arbitrary"), vmem_limit_bytes=VMEM_LIMIT),
        name="mla_sample",
    )(page_table, *caches, *seq_in, *consts)


def _select_sample_kernel(sc_ref, qi_ref, ki_ref, wi_ref, tri_ref, bias_ref, *, topk):
    n = sc_ref.shape[0]
    ki = ki_ref[...].astype(F32)
    wi = wi_ref[...] * IDX_SCALE
    s_new = jnp.zeros((n, 1), F32)
    for h in range(IDX_HEADS):
        d = jnp.sum(qi_ref[:, h * IDX_DIM:(h + 1) * IDX_DIM].astype(F32) * ki, axis=1, keepdims=True)
        s_new = s_new + jnp.maximum(d, 0.0) * wi[:, h:h + 1]
    new_lane = lax.broadcasted_iota(jnp.int32, (n, LANES), 1) == 0
    tail = jnp.where(new_lane, s_new, -jnp.inf)
    s = jnp.concatenate([sc_ref[...], tail], axis=1)
    valid = lax.broadcasted_iota(jnp.int32, s.shape, 1) <= sc_ref.shape[1]
    sel = _topk_mask(s, topk, tri_ref) & valid
    bias_ref[...] = jnp.where(sel, 0.0, NEG_BIG)


def _select_sample(scores, qi, ki, wi, tri, topk):
    n, p = scores.shape
    return pl.pallas_call(
        functools.partial(_select_sample_kernel, topk=topk),
        out_shape=jax.ShapeDtypeStruct((n, p + LANES), F32),
        compiler_params=pltpu.CompilerParams(vmem_limit_bytes=VMEM_LIMIT),
        name="select_sample",
    )(scores, qi, ki, wi, tri)


def _dsa_sample_kernel(pt_ref, k_hbm, v_hbm, bias_ref, bnew_ref, qd8_ref, kn8_ref, vn8_ref, o_ref, m_ref, l_ref, acc_ref,
                       k_buf, v_buf, sems, *, n_pages):
    j = pl.program_id(1)
    group = DSA_HEADS // DSA_KV_HEADS
    slot = _stream_pages(pt_ref, (k_hbm, v_hbm), (k_buf, v_buf), sems, n_pages)

    @pl.when(j == 0)
    def _():
        m_ref[...] = jnp.full(m_ref.shape, NEG_BIG, F32)
        l_ref[...] = jnp.zeros(l_ref.shape, F32)
        acc_ref[...] = jnp.zeros(acc_ref.shape, F32)

    q = qd8_ref[0]
    ch = n_pages * PAGE_SIZE
    kv_head = lax.broadcasted_iota(jnp.int32, (DSA_HEADS, ch), 0) // group
    s = jnp.zeros((DSA_HEADS, ch), F32)
    for n in range(DSA_KV_HEADS):
        k_t = jnp.concatenate([k_buf[slot, p, n] for p in range(n_pages)], axis=1).astype(BF16)
        s = jnp.where(kv_head == n, _dot(q, k_t), s)
    s = s + bias_ref[0]
    m_new = jnp.maximum(m_ref[...], jnp.max(s, axis=1, keepdims=True))
    a = jnp.exp(m_ref[...] - m_new)
    p = jnp.exp(s - m_new)
    l_ref[...] = a * l_ref[...] + jnp.sum(p, axis=1, keepdims=True)
    m_ref[...] = m_new
    pb = p.astype(BF16)
    for n in range(DSA_KV_HEADS):
        v_t = jnp.concatenate([v_buf[slot, p, n] for p in range(n_pages)], axis=1).astype(BF16)
        acc_ref[n] = a * acc_ref[n] + _dot_nt(pb, v_t)

    @pl.when(j == pl.num_programs(1) - 1)
    def _():
        s_new = jnp.sum(q.astype(F32) * kn8_ref[0].astype(BF16).astype(F32), axis=1, keepdims=True) + bnew_ref[0]
        m_f = jnp.maximum(m_ref[...], s_new)
        a_f = jnp.exp(m_ref[...] - m_f)
        p_new = jnp.exp(s_new - m_f)
        l_f = a_f * l_ref[...] + p_new
        row_head = lax.broadcasted_iota(jnp.int32, (DSA_HEADS, DSA_HEAD_DIM), 0) // group
        o = jnp.zeros((DSA_HEADS, DSA_HEAD_DIM), F32)
        for n in range(DSA_KV_HEADS):
            o = o + jnp.where(row_head == n, acc_ref[n], 0.0)
        o_ref[0] = (a_f * o + p_new * vn8_ref[0]) / l_f


def _dsa_sample(page_table, cache_k, cache_v, bias, bias_new, qd8, kn8, vn8, n_pages):
    nb, n_tot = page_table.shape
    n_chunks = n_tot // n_pages
    ch = n_pages * PAGE_SIZE

    def seq_spec(a):
        return pl.BlockSpec((1,) + a.shape[1:], lambda b, j, pt: (b, 0, 0))

    caches = [cache_k, cache_v]
    in_specs = ([pl.BlockSpec(memory_space=pl.ANY)] * len(caches) + [pl.BlockSpec((1, 1, ch), lambda b, j, pt: (b, 0, j))]
                + [seq_spec(a) for a in (bias_new, qd8, kn8, vn8)])
    grid_spec = pltpu.PrefetchScalarGridSpec(
        num_scalar_prefetch=1, grid=(nb, n_chunks), in_specs=in_specs,
        out_specs=pl.BlockSpec((1, DSA_HEADS, DSA_HEAD_DIM), lambda b, j, pt: (b, 0, 0)),
        scratch_shapes=[pltpu.VMEM((DSA_HEADS, 1), F32), pltpu.VMEM((DSA_HEADS, 1), F32),
                        pltpu.VMEM((DSA_KV_HEADS, DSA_HEADS, DSA_HEAD_DIM), F32)]
        + [pltpu.VMEM((2, n_pages) + a.shape[2:], a.dtype) for a in caches]
        + [pltpu.SemaphoreType.DMA((2, len(caches)))])
    return pl.pallas_call(
        functools.partial(_dsa_sample_kernel, n_pages=n_pages),
        grid_spec=grid_spec,
        out_shape=jax.ShapeDtypeStruct((nb, DSA_HEADS, DSA_HEAD_DIM), F32),
        compiler_params=pltpu.CompilerParams(dimension_semantics=("arbitrary", "arbitrary"), vmem_limit_bytes=VMEM_LIMIT),
        name="dsa_sample",
    )(page_table, *caches, bias, bias_new, qd8, kn8, vn8)


def _pad_heads(w, lead, n_heads, width, offset=0):
    w = w.reshape(lead + (n_heads, width))
    w = jnp.pad(w, [(0, 0)] * len(lead) + [(0, 0), (offset, HEAD_PAD - width - offset)])
    return w.reshape(lead + (n_heads * HEAD_PAD,))


def _block_diag_ones(width):
    r = jnp.arange(MXU_COLS) // width
    return (r[:, None] == r[None, :]).astype(BF16)


def _prepare_weights(l, g_in, w_in, g_q_lat, w_uq, g_mla_q, g_kv_lat, w_uk, w_uv, g_mla_k, g_dsa_q, g_dsa_k, w_branch_a,
                     w_branch_b, w_out, g_ffn, w_gate, w_up, w_down):
    d = w_in.shape[1]
    splits = (MLA_Q_LORA, MLA_KV_LORA, MLA_ROPE, DSA_HEADS * DSA_HEAD_DIM, DSA_KV_HEADS * DSA_HEAD_DIM,
              DSA_KV_HEADS * DSA_HEAD_DIM, IDX_HEADS * IDX_DIM, IDX_DIM, IDX_HEADS, d, d)
    pts, acc = [], 0
    for s in splits[:-1]:
        acc += s
        pts.append(acc)
    (w_ql, w_kv, w_pe, w_qd, w_kd, w_vd, w_qi, w_ki, w_wi, w_ga, w_gb) = jnp.split(w_in[l], pts, axis=1)
    w_small = jnp.concatenate([w_pe, w_ki, w_wi, jnp.zeros((d, LANES - _S_WI - IDX_HEADS), F32)], axis=1)
    row = lambda g: g.reshape(1, -1).astype(F32)
    eye_pe = jnp.eye(MLA_ROPE, dtype=F32)
    half_pe = MLA_ROPE // 2
    return {
        "g_in": row(g_in[l]),
        "wa": jnp.concatenate([w_ql, w_kv, w_qd, w_kd, w_vd, w_qi, w_small], axis=1).astype(BF16),
        "wg": jnp.concatenate([w_ga, w_gb], axis=1).astype(BF16),
        "g_q_lat": row(g_q_lat[l]),
        "w_uq": _pad_heads(w_uq[l].reshape(MLA_Q_LORA, -1), (MLA_Q_LORA,), MLA_HEADS, MLA_QK).astype(BF16),
        "g_mla_q": row(_pad_heads(jnp.tile(g_mla_q[l], MLA_HEADS), (), MLA_HEADS, MLA_QK)),
        "g_kv_lat": row(g_kv_lat[l]),
        "w_uk": _pad_heads(w_uk[l].reshape(MLA_KV_LORA, -1), (MLA_KV_LORA,), MLA_HEADS, MLA_NOPE, MLA_ROPE).astype(BF16),
        "t_pe": _pad_heads(jnp.tile(eye_pe, (1, MLA_HEADS)), (MLA_ROPE,), MLA_HEADS, MLA_ROPE).astype(BF16),
        "w_uv": _pad_heads(w_uv[l].reshape(MLA_KV_LORA, -1), (MLA_KV_LORA,), MLA_HEADS, MLA_V).astype(BF16),
        "g_mla_k": row(_pad_heads(jnp.tile(g_mla_k[l], MLA_HEADS), (), MLA_HEADS, MLA_QK)),
        "g_dsa_q": row(jnp.tile(g_dsa_q[l], DSA_HEADS)),
        "g_dsa_k": row(jnp.tile(g_dsa_k[l], DSA_KV_HEADS)),
        "g_mla_k_swap": row(_pad_heads(jnp.tile(jnp.concatenate([g_mla_k[l][half_pe:MLA_ROPE], g_mla_k[l][:half_pe]]),
                                                MLA_HEADS), (), MLA_HEADS, MLA_ROPE)),
        "w_uk_t": jnp.transpose(w_uk[l], (2, 1, 0)).reshape(MLA_HEADS * MLA_NOPE, MLA_KV_LORA).astype(BF16),
        "eye_pe": eye_pe.astype(BF16),
        "p_same": jnp.pad(eye_pe, ((0, HEAD_PAD - MLA_ROPE), (0, MLA_ROPE))).astype(BF16),
        "p_cross": jnp.pad(jnp.concatenate([jnp.roll(eye_pe, half_pe, axis=0)[:, :half_pe],
                                            -jnp.roll(eye_pe, half_pe, axis=0)[:, half_pe:]], axis=1),
                           ((0, HEAD_PAD - MLA_ROPE), (MLA_ROPE, 0))).astype(BF16),
        "bd128": _block_diag_ones(HEAD_PAD),
        "bd64": _block_diag_ones(DSA_HEAD_DIM),
        "w_branch_a": _pad_heads(w_branch_a[l].T, (d,), MLA_HEADS, MLA_V).T.astype(BF16),
        "w_branch_b": _pad_heads(w_branch_b[l].T, (d,), DSA_HEADS, DSA_HEAD_DIM).T.astype(BF16),
        "w_out": w_out[l].astype(BF16),
        "g_ffn": row(g_ffn[l]),
        "w_gate": w_gate[l].astype(BF16),
        "w_up": w_up[l].astype(BF16),
        "w_down": w_down[l].astype(BF16),
    }


def _head_tables(pos, rot, width):
    half = rot // 2
    inv_freq = ROPE_THETA ** (-jnp.arange(half, dtype=F32) / half)
    ang = pos.astype(F32)[:, None] * inv_freq[None, :]
    cos, sin = jnp.cos(ang), jnp.sin(ang)
    t = pos.shape[0]
    c = jnp.concatenate([cos, cos, jnp.ones((t, width - rot), F32)], axis=1)
    s_lo = jnp.concatenate([-sin, jnp.zeros((t, width - half), F32)], axis=1)
    s_hi = jnp.concatenate([jnp.zeros((t, half), F32), sin, jnp.zeros((t, width - rot), F32)], axis=1)
    return c, s_lo, s_hi


def _rope_tables(pos):
    t = pos.shape[0]
    cm, slm, shm = (jnp.tile(a, (1, MLA_HEADS)) for a in _head_tables(pos, MLA_ROPE, HEAD_PAD))
    cd, sld, shd = (jnp.tile(a, (1, DSA_HEADS)) for a in _head_tables(pos, DSA_ROT, DSA_HEAD_DIM))
    ci, sli, shi = _head_tables(pos, IDX_ROT, IDX_DIM)
    pad = lambda a, fill: jnp.concatenate(
        [jnp.full((t, _S_KI), fill, F32), a, jnp.full((t, LANES - _S_WI), fill, F32)], axis=1)
    return {"cm": cm, "slm": slm, "shm": shm, "cd": cd, "sld": sld, "shd": shd,
            "cs": pad(ci, 1.0), "sls": pad(sli, 0.0), "shs": pad(shi, 0.0)}


def _upper_tri():
    r = jnp.arange(LANES)
    return (r[:, None] <= r[None, :]).astype(BF16)


def _prompt_layer(xp, wts, pad_front, topk, n_out_rows):
    T = xp.shape[1]
    tabs = _rope_tables(jnp.arange(T) - pad_front)
    qm, ckv, km, vm, qd, kd, kdb, vd, vdb, qi, sm = _project(xp, wts, tabs)
    om = _mla_prompt(qm, km, vm, pad_front, n_out_rows)
    kib = sm[:, :, _S_KI:_S_WI].astype(BF16)
    wi = sm[:, :, _S_WI:_S_WI + IDX_HEADS]
    od = _dsa_prompt(qi, wi, kib, qd, kdb, vdb, _upper_tri(), pad_front, topk, n_out_rows)
    y = _merge_ffn(xp, om, od, wts, T - n_out_rows)
    return y, ckv, sm, kd, vd


def _past_tables(past, n_pages):
    half = MLA_ROPE // 2
    inv_freq = ROPE_THETA ** (-jnp.arange(half, dtype=F32) / half)
    ang = jnp.arange(past).astype(F32)[:, None] * inv_freq[None, :]
    ch = n_pages * PAGE_SIZE

    def chunked(a):
        a = jnp.concatenate([a, a], axis=1).T.reshape(MLA_ROPE, past // ch, ch)
        return jnp.transpose(a, (1, 0, 2))

    return chunked(jnp.cos(ang)), chunked(jnp.sin(ang))


def _sample_layer(xs, wts, page_table, cache_lat, cache_kpe, cache_dk, cache_dv, cache_ik, topk):
    n, d = xs.shape
    n_tot = page_table.shape[1]
    past = n_tot * PAGE_SIZE
    n_pages = PAGES_PER_STEP if n_tot % PAGES_PER_STEP == 0 else n_tot
    assert n_pages % 2 == 0
    cache_kpe = jnp.swapaxes(cache_kpe, 2, 3)
    cache_ik = jnp.swapaxes(cache_ik, 2, 3)
    cache_dk = jnp.transpose(cache_dk, (0, 1, 3, 4, 2))
    cache_dv = jnp.transpose(cache_dv, (0, 1, 3, 4, 2))
    x3 = xs.reshape(1, n, d)
    qm, ckv, km, vm, qd, kd, _, vd, _, qi, sm = _project(x3, wts, _rope_tables(jnp.full((n,), past, jnp.int32)))
    qabs, qq = _absorb(qm[0], wts)
    per_head = lambda a, w: a[0].reshape(n, a.shape[2] // w, w)
    ki_new = sm[0, :, _S_KI:_S_WI].astype(BF16)
    wi_new = sm[0, :, _S_WI:_S_WI + IDX_HEADS]
    cos_t, sin_t = _past_tables(past, n_pages)
    qabs = jnp.pad(qabs, ((0, 2 * MLA_HEADS - qabs.shape[0]), (0, 0), (0, 0)))
    om8, scores = _mla_sample(page_table, cache_lat, cache_kpe, cache_ik, jnp.transpose(qabs, (1, 0, 2)),
                              jnp.transpose(qq, (1, 0, 2)), per_head(qm, HEAD_PAD), per_head(km, HEAD_PAD),
                              per_head(vm, HEAD_PAD), per_head(qi, IDX_DIM), wi_new.reshape(n, IDX_HEADS, 1), wts,
                              cos_t, sin_t, n_pages)
    bias = _select_sample(scores.reshape(n, past), qi[0], ki_new, wi_new, _upper_tri(), topk)
    group = DSA_HEADS // DSA_KV_HEADS
    od8 = _dsa_sample(page_table, cache_dk, cache_dv, bias[:, :past].reshape(n, 1, past),
                      bias[:, past:past + 1].reshape(n, 1, 1), per_head(qd, DSA_HEAD_DIM),
                      jnp.repeat(per_head(kd, DSA_HEAD_DIM), group, axis=1),
                      jnp.repeat(per_head(vd, DSA_HEAD_DIM), group, axis=1), n_pages)
    od = jnp.pad(od8, ((0, 0), (0, 0), (0, HEAD_PAD - DSA_HEAD_DIM))).astype(BF16)
    y = _merge_ffn(x3, om8.reshape(1, n, MLA_HEADS * HEAD_PAD), od.reshape(1, n, DSA_HEADS * HEAD_PAD), wts, 0)
    return y[0], ckv[0], sm[0], kd[0], vd[0]


def kernel(x_prompt, x_sample, cache_mla_latent, cache_mla_kpe, cache_dsa_k, cache_dsa_v, cache_idx_k, page_table,
           meta_tokens, g_in, w_in, g_q_lat, w_uq, g_mla_q, g_kv_lat, w_uk, w_uv, g_mla_k, g_dsa_q, g_dsa_k, w_branch_a,
           w_branch_b, w_out, g_ffn, w_gate, w_up, w_down):
    B, S, D = x_prompt.shape
    L = S + N_META
    pad_front = (-N_META) % Q_TILE
    topk_prompt = min(TOPK_MAX, L // 4)
    wts = _prepare_weights(0, g_in, w_in, g_q_lat, w_uq, g_mla_q, g_kv_lat, w_uk, w_uv, g_mla_k, g_dsa_q, g_dsa_k,
                           w_branch_a, w_branch_b, w_out, g_ffn, w_gate, w_up, w_down)
    meta = jnp.broadcast_to(meta_tokens.astype(F32)[None], (B, N_META, D))
    xp = jnp.concatenate([jnp.zeros((B, pad_front, D), F32), meta, x_prompt], axis=1)
    y_prompt, ckv, sm, kd, vd = _prompt_layer(xp, wts, pad_front, topk_prompt, S)
    real = slice(pad_front, pad_front + L)

    N, T, _ = x_sample.shape
    assert T == 1 and g_in.shape[0] == 1, "one trunk layer and one new token per running sequence"
    past = page_table.shape[1] * PAGE_SIZE
    topk_sample = min(TOPK_MAX, (past + T) // 4)
    y_s, ckv_s, sm_s, kd_s, vd_s = _sample_layer(x_sample[:, 0], wts, page_table, cache_mla_latent, cache_mla_kpe,
                                                 cache_dsa_k, cache_dsa_v, cache_idx_k, topk_sample)
    heads = (DSA_KV_HEADS, DSA_HEAD_DIM)
    return (y_prompt, y_s[:, None],
            ckv[None, :, real], ckv_s[None, :, None],
            sm[None, :, real, _S_KPE:_S_KI], sm_s[None, :, None, _S_KPE:_S_KI],
            kd[None, :, real].reshape((1, B, L) + heads), kd_s.reshape((1, N, 1) + heads),
            vd[None, :, real].reshape((1, B, L) + heads), vd_s.reshape((1, N, 1) + heads),
            sm[None, :, real, _S_KI:_S_WI], sm_s[None, :, None, _S_KI:_S_WI])
```
